```python
import jax, jax.numpy as jnp
from jax import lax
import numpy as np

D_MODEL = 2048
BATCH = 4
SEQ = 4096
DEPTH = 4

GRID_W = 64
CTX_LEN = 256
MIX_W = D_MODEL

ML_HEADS = 4
ML_DV = MIX_W // 2 // ML_HEADS
ML_DK = ML_DV // 2
ML_VW = ML_HEADS * ML_DV
ML_KW = ML_HEADS * ML_DK
ML_GATES = 2 * 2 * ML_HEADS
ML_CHUNK = 128
QK_CONV = 3
M_INIT = -1e30

SC_W = MIX_W - ML_VW
SC_CONV = 3

GM_W = MIX_W // 2
GM_GROUPS = 4
GM_CHUNK = 128

CF_W = MIX_W - GM_W
CF_CONV = 31

D_FF = 5632
FFN_CONV = 3

LN_EPS = 1e-5

EV_K0 = 0
EV_V0 = EV_K0 + ML_KW
EV_G0 = EV_V0 + ML_VW
EV_STATE_COLS = EV_G0 + ML_GATES
EV_Q0 = EV_STATE_COLS
EV_O0 = EV_Q0 + ML_KW
EV_SC0 = EV_O0 + ML_VW
EV_IN = EV_SC0 + 3 * SC_W
OD_IN = 2 * GM_W + 2 * CF_W

kernel_name = 'hybrid_mlstm_shortconv_gmlp_conformer_dit'


def layer_norm(z, g, b):
    zf = z.astype(jnp.float32)
    mu = zf.mean(-1, keepdims=True)
    var = jnp.square(zf - mu).mean(-1, keepdims=True)
    return ((zf - mu) * lax.rsqrt(var + LN_EPS)).astype(z.dtype) * g + b


def dwconv_seq(z, w):
    k = w.shape[0]
    return lax.conv_general_dilated(z, w[:, None, :].astype(z.dtype), (1,), [(k // 2, k // 2)],
                                    dimension_numbers=('NWC', 'WIO', 'NWC'),
                                    feature_group_count=z.shape[-1])


def dwconv_rows(z, w, rows):
    b, t, ch = z.shape
    return dwconv_seq(z.reshape(b * rows, GRID_W, ch), w).reshape(b, t, ch)


def dwconv_cols(z, w, rows):
    b, t, ch = z.shape
    zc = z.reshape(b, rows, GRID_W, ch).transpose(0, 2, 1, 3).reshape(b * GRID_W, rows, ch)
    return dwconv_seq(zc, w).reshape(b, GRID_W, rows, ch).transpose(0, 2, 1, 3).reshape(b, t, ch)


def conv_short(z, w, rows):
    return dwconv_seq(z, w) if rows is None else dwconv_rows(z, w, rows)


def conv_long(z, w, rows):
    return dwconv_seq(z, w) if rows is None else dwconv_cols(z, w, rows)


def modulation(cvec, w, b):
    m = jax.nn.silu(cvec) @ w + b
    return m.reshape(-1, 1, 6, D_MODEL)


def ada(z, m, i):
    return z * (1.0 + m[..., i + 1, :]) + m[..., i, :]


def _chunk(z, n):
    b, t, h = z.shape[:3]
    z = z.reshape((b, n, t // n, h) + z.shape[3:])
    return jnp.swapaxes(z, 2, 3)


def _zero_state(bsz):
    return (jnp.zeros((bsz, ML_HEADS, ML_DV, ML_DK), jnp.float32),
            jnp.zeros((bsz, ML_HEADS, ML_DK), jnp.float32),
            jnp.full((bsz, ML_HEADS), M_INIT, jnp.float32))


def _dir(z, d):
    return jnp.flip(z, axis=1) if d == 1 else z


def mlstm_states(k, v, ig, lf, state0):
    n = k.shape[1] // ML_CHUNK
    k, v, ig, lf = (_chunk(z, n) for z in (k, v, ig, lf))
    b = jnp.cumsum(lf, axis=-1)
    b_last = b[..., -1]
    a = b_last[..., None] - b + ig
    m_loc = a.max(-1)
    w = jnp.exp(a - m_loc[..., None])
    c_loc = jnp.einsum('bnhs,bnhsv,bnhsk->bnhvk', w, v, k)
    n_loc = jnp.einsum('bnhs,bnhsk->bnhk', w, k)

    def step(carry, inp):
        c_prev, n_prev, m_prev = carry
        c_l, n_l, m_l, b_l = inp
        m_new = jnp.maximum(b_l + m_prev, m_l)
        d_prev = jnp.exp(b_l + m_prev - m_new)
        d_loc = jnp.exp(m_l - m_new)
        c_new = d_prev[..., None, None] * c_prev + d_loc[..., None, None] * c_l
        n_new = d_prev[..., None] * n_prev + d_loc[..., None] * n_l
        return (c_new, n_new, m_new), carry

    xs = tuple(jnp.moveaxis(z, 1, 0) for z in (c_loc, n_loc, m_loc, b_last))
    final, starts = lax.scan(step, state0, xs)
    starts = tuple(jnp.moveaxis(z, 0, 1) for z in starts)
    return starts, final


def mlstm_outputs(q, k, v, ig, lf, starts):
    n = q.shape[1] // ML_CHUNK
    q, k, v, ig, lf = (_chunk(z, n) for z in (q, k, v, ig, lf))
    c_s, n_s, m_s = starts
    b = jnp.cumsum(lf, axis=-1)
    causal = jnp.tril(jnp.ones((ML_CHUNK, ML_CHUNK), bool))
    log_intra = jnp.where(causal, b[..., :, None] - b[..., None, :] + ig[..., None, :], -jnp.inf)
    log_inter = b + m_s[..., None]
    m_t = jnp.maximum(log_inter, log_intra.max(-1))
    s = jnp.einsum('bnhtk,bnhsk->bnhts', q, k) * jnp.exp(log_intra - m_t[..., None])
    w_inter = jnp.exp(log_inter - m_t)
    num = jnp.einsum('bnhts,bnhsv->bnhtv', s, v) + w_inter[..., None] * jnp.einsum('bnhtk,bnhvk->bnhtv', q, c_s)
    den = s.sum(-1) + w_inter * jnp.einsum('bnhtk,bnhk->bnht', q, n_s)
    h = num / jnp.maximum(jnp.abs(den), jnp.exp(-m_t))[..., None]
    bsz, _, hh, _, dv = h.shape
    return jnp.swapaxes(h, 2, 3).reshape(bsz, n * ML_CHUNK, hh, dv)


def mlstm_kvg(proj, conv_k, b_gates, rows):
    bsz, t, _ = proj.shape
    k = jax.nn.silu(conv_short(proj[..., EV_K0:EV_V0], conv_k, rows)) * (ML_DK ** -0.5)
    v = proj[..., EV_V0:EV_G0]
    g = (proj[..., EV_G0:EV_STATE_COLS] + b_gates).astype(jnp.float32).reshape(bsz, t, 2, 2, ML_HEADS)
    k = k.astype(jnp.float32).reshape(bsz, t, ML_HEADS, ML_DK)
    v = v.astype(jnp.float32).reshape(bsz, t, ML_HEADS, ML_DV)
    return k, v, g[:, :, :, 0], jax.nn.log_sigmoid(g[:, :, :, 1])


def mlstm_ctx_states(proj, conv_k, b_gates):
    k, v, ig, lf = mlstm_kvg(proj, conv_k, b_gates, None)
    state0 = _zero_state(k.shape[0])
    return [mlstm_states(_dir(k, d), _dir(v, d), _dir(ig[:, :, d], d), _dir(lf[:, :, d], d), state0)[1]
            for d in range(2)]


def even_stream(proj, conv_k, conv_q, b_gates, norm_g, conv_sc, w_out, rows, init):
    bsz, t, _ = proj.shape
    k, v, ig, lf = mlstm_kvg(proj, conv_k, b_gates, rows)
    q = jax.nn.silu(conv_short(proj[..., EV_Q0:EV_O0], conv_q, rows))
    q = q.astype(jnp.float32).reshape(bsz, t, ML_HEADS, ML_DK)
    o = proj[..., EV_O0:EV_SC0]
    hs = []
    finals = []
    for d in range(2):
        qd, kd, vd, igd, lfd = (_dir(z, d) for z in (q, k, v, ig[:, :, d], lf[:, :, d]))
        starts, final = mlstm_states(kd, vd, igd, lfd, init[d])
        hs.append(_dir(mlstm_outputs(qd, kd, vd, igd, lfd, starts), d))
        finals.append(final)
    h = hs[0] + hs[1]
    mu = h.mean(-1, keepdims=True)
    var = jnp.square(h - mu).mean(-1, keepdims=True)
    hn = ((h - mu) * lax.rsqrt(var + LN_EPS)).reshape(bsz, t, ML_VW).astype(o.dtype) * norm_g
    y_ml = hn * jax.nn.sigmoid(o)
    gate_c = proj[..., EV_SC0:EV_SC0 + SC_W]
    gate_b = proj[..., EV_SC0 + SC_W:EV_SC0 + 2 * SC_W]
    val = proj[..., EV_SC0 + 2 * SC_W:]
    y_sc = gate_b * conv_short(gate_c * val, conv_sc, rows)
    return jnp.concatenate([y_ml, y_sc], axis=-1) @ w_out, finals


def odd_stream(proj, sgu_g, sgu_b, w_s, b_s, conv_dw, cf_g, cf_b, w_out, rows):
    bsz, t, _ = proj.shape
    uv = jax.nn.gelu(proj[..., :2 * GM_W], approximate=False)
    u, v = uv[..., :GM_W], uv[..., GM_W:]
    v = layer_norm(v, sgu_g, sgu_b)
    n = t // GM_CHUNK
    v = v.reshape(bsz, n, GM_CHUNK, GM_GROUPS, GM_W // GM_GROUPS)
    v = jnp.einsum('gts,bnsgc->bntgc', w_s, v) + b_s.T[:, :, None]
    y_gm = u * v.reshape(bsz, t, GM_W)
    glu = proj[..., 2 * GM_W:2 * GM_W + CF_W] * jax.nn.sigmoid(proj[..., 2 * GM_W + CF_W:])
    y_cf = jax.nn.silu(layer_norm(conv_long(glu, conv_dw, rows), cf_g, cf_b))
    return jnp.concatenate([y_gm, y_cf], axis=-1) @ w_out


def conv_ffn(h, w_up, w_conv, w_down, rows):
    up = conv_short(h @ w_up, w_conv, rows)
    a, g = up[..., :D_FF], up[..., D_FF:]
    return (jax.nn.silu(g) * a) @ w_down


def setup_inputs(seed: int = 0) -> dict:
    key = jax.random.key(seed)
    ks = iter(jax.random.split(key, 40))
    d = D_MODEL
    ne = (DEPTH + 1) // 2
    no = DEPTH // 2
    beta = (8.0 * DEPTH) ** -0.25

    def nrm(shape, scale):
        return jax.random.normal(next(ks), shape, jnp.float32) * scale

    def gain(shape):
        return 1.0 + nrm(shape, 0.02)

    b_i = nrm((ne, 2, 1, ML_HEADS), 0.1)
    b_f = 3.0 + nrm((ne, 2, 1, ML_HEADS), 0.5)
    ev_b_gates = jnp.concatenate([b_i, b_f], axis=2).reshape(ne, ML_GATES)
    return {
        'x': nrm((BATCH, SEQ, d), 1.0),
        'c': nrm((BATCH, d), 1.0),
        'ctx': nrm((BATCH, CTX_LEN, d), 1.0),
        'c_ctx': nrm((d,), 1.0),
        'w_mod': nrm((DEPTH, d, 6 * d), 0.5 * d ** -0.5),
        'b_mod': nrm((DEPTH, 6 * d), 0.02),
        'ln1_g': gain((DEPTH, d)),
        'ln1_b': nrm((DEPTH, d), 0.02),
        'ln2_g': gain((DEPTH, d)),
        'ln2_b': nrm((DEPTH, d), 0.02),
        'ffn_w_up': nrm((DEPTH, d, 2 * D_FF), d ** -0.5),
        'ffn_conv': nrm((DEPTH, FFN_CONV, 2 * D_FF), FFN_CONV ** -0.5),
        'ffn_w_down': nrm((DEPTH, D_FF, d), beta * D_FF ** -0.5),
        'ev_w_in': nrm((ne, d, EV_IN), d ** -0.5),
        'ev_b_gates': ev_b_gates,
        'ev_conv_k': nrm((ne, QK_CONV, ML_KW), QK_CONV ** -0.5),
        'ev_conv_q': nrm((ne, QK_CONV, ML_KW), QK_CONV ** -0.5),
        'ev_norm_g': gain((ne, ML_VW)),
        'ev_conv_sc': nrm((ne, SC_CONV, SC_W), SC_CONV ** -0.5),
        'ev_w_out': nrm((ne, MIX_W, d), beta * MIX_W ** -0.5),
        'od_w_in': nrm((no, d, OD_IN), d ** -0.5),
        'od_sgu_g': gain((no, GM_W)),
        'od_sgu_b': nrm((no, GM_W), 0.02),
        'od_w_s': nrm((no, GM_GROUPS, GM_CHUNK, GM_CHUNK), GM_CHUNK ** -0.5),
        'od_b_s': 1.0 + nrm((no, GM_GROUPS, GM_CHUNK), 0.02),
        'od_conv_dw': nrm((no, CF_CONV, CF_W), CF_CONV ** -0.5),
        'od_cf_g': gain((no, CF_W)),
        'od_cf_b': nrm((no, CF_W), 0.02),
        'od_w_out': nrm((no, MIX_W, d), beta * MIX_W ** -0.5),
    }


def reference(x, c, ctx, c_ctx, w_mod, b_mod, ln1_g, ln1_b, ln2_g, ln2_b, ffn_w_up, ffn_conv, ffn_w_down,
              ev_w_in, ev_b_gates, ev_conv_k, ev_conv_q, ev_norm_g, ev_conv_sc, ev_w_out,
              od_w_in, od_sgu_g, od_sgu_b, od_w_s, od_b_s, od_conv_dw, od_cf_g, od_cf_b, od_w_out):
    rows = x.shape[1] // GRID_W
    alpha = (2.0 * DEPTH) ** 0.25
    last_even = ((DEPTH - 1) // 2) * 2
    for l in range(DEPTH):
        need_ctx = l <= last_even
        ctx_out = l < last_even
        j = l // 2
        mx = modulation(c, w_mod[l], b_mod[l])
        hx = ada(x, mx, 0)
        if need_ctx:
            mc = modulation(c_ctx, w_mod[l], b_mod[l])
            hc = ada(ctx, mc, 0)
        if l % 2 == 0:
            if ctx_out:
                yc, fin = even_stream(hc @ ev_w_in[j], ev_conv_k[j], ev_conv_q[j], ev_b_gates[j], ev_norm_g[j],
                                      ev_conv_sc[j], ev_w_out[j], None, [_zero_state(ctx.shape[0])] * 2)
            else:
                fin = mlstm_ctx_states(hc @ ev_w_in[j][:, :EV_STATE_COLS], ev_conv_k[j], ev_b_gates[j])
            yx, _ = even_stream(hx @ ev_w_in[j], ev_conv_k[j], ev_conv_q[j], ev_b_gates[j], ev_norm_g[j],
                                ev_conv_sc[j], ev_w_out[j], rows, fin)
        else:
            yx = odd_stream(hx @ od_w_in[j], od_sgu_g[j], od_sgu_b[j], od_w_s[j], od_b_s[j], od_conv_dw[j],
                            od_cf_g[j], od_cf_b[j], od_w_out[j], rows)
            if ctx_out:
                yc = odd_stream(hc @ od_w_in[j], od_sgu_g[j], od_sgu_b[j], od_w_s[j], od_b_s[j], od_conv_dw[j],
                                od_cf_g[j], od_cf_b[j], od_w_out[j], None)
        x = layer_norm(alpha * x + mx[..., 2, :] * yx, ln1_g[l], ln1_b[l])
        x = layer_norm(alpha * x + mx[..., 5, :] * conv_ffn(ada(x, mx, 3), ffn_w_up[l], ffn_conv[l],
                                                            ffn_w_down[l], rows), ln2_g[l], ln2_b[l])
        if ctx_out:
            ctx = layer_norm(alpha * ctx + mc[..., 2, :] * yc, ln1_g[l], ln1_b[l])
            ctx = layer_norm(alpha * ctx + mc[..., 5, :] * conv_ffn(ada(ctx, mc, 3), ffn_w_up[l], ffn_conv[l],
                                                                    ffn_w_down[l], None), ln2_g[l], ln2_b[l])
    return x
```

```python
import functools

import jax
import jax.numpy as jnp
from jax import lax
from jax.experimental import pallas as pl
from jax.experimental.pallas import tpu as pltpu

F32 = jnp.float32
BF16 = jnp.bfloat16

D_MODEL = 2048
GRID_W = 64
ML_HEADS = 4
ML_DV = 256
ML_DK = 128
ML_VW = ML_HEADS * ML_DV
ML_KW = ML_HEADS * ML_DK
ML_GATES = 16
ML_CHUNK = 128
M_INIT = -1e30
SC_W = 1024
GM_W = 1024
GM_GROUPS = 4
GM_CHUNK = 128
CF_W = 1024
CF_CONV = 31
D_FF = 5632
LN_EPS = 1e-5

EV_V0 = ML_KW
EV_G0 = EV_V0 + ML_VW
EV_Q0 = EV_G0 + ML_GATES
EV_O0 = EV_Q0 + ML_KW
EV_SC0 = EV_O0 + ML_VW

V7X_VMEM_BYTES = 64 * 2**20
VMEM_LIMIT = V7X_VMEM_BYTES - 8 * 2**20
LANES = 128

TM_LATENT = 512
FFN_CHUNK = 512
CONV_ROWS = 64


def _cparams(sem):
    return pltpu.CompilerParams(dimension_semantics=sem, vmem_limit_bytes=VMEM_LIMIT)


def _silu(z):
    return z * jax.nn.sigmoid(z)


def _gelu(z):
    return 0.5 * z * (1.0 + lax.erf(z * (2.0 ** -0.5)))


def _log_sigmoid(z):
    return jnp.minimum(z, 0.0) - jnp.log1p(jnp.exp(-jnp.abs(z)))


def _ln(z, g, b):
    mu = jnp.mean(z, axis=-1, keepdims=True)
    zc = z - mu
    var = jnp.mean(zc * zc, axis=-1, keepdims=True)
    return zc * lax.rsqrt(var + LN_EPS) * g + b


def _ada(x, mod_ref, slot):
    return x * (1.0 + mod_ref[0, slot + 1:slot + 2, :]) + mod_ref[0, slot:slot + 1, :]


def _conv3(a, w, row_w):
    tm = a.shape[0]
    t = lax.broadcasted_iota(jnp.int32, (tm, 1), 0) % row_w
    prev = jnp.where(t == 0, 0.0, pltpu.roll(a, 1, 0))
    nxt = jnp.where(t == row_w - 1, 0.0, pltpu.roll(a, tm - 1, 0))
    return w[0:1, :] * prev + w[1:2, :] * a + w[2:3, :] * nxt


def _dot(a, b):
    return jnp.dot(a, b, preferred_element_type=F32)


def _mod_kernel(c_ref, w_ref, b_ref, o_ref):
    s = _silu(c_ref[...]).astype(BF16)
    o_ref[0] = _dot(s, w_ref[0].astype(BF16)) + b_ref[0]


def _modulation(c8, w_mod, b_mod):
    depth, d, n = w_mod.shape
    tn = 1024
    return pl.pallas_call(
        _mod_kernel,
        grid=(depth, n // tn),
        in_specs=[
            pl.BlockSpec((8, d), lambda l, j: (0, 0)),
            pl.BlockSpec((1, d, tn), lambda l, j: (l, 0, j)),
            pl.BlockSpec((1, 1, tn), lambda l, j: (l, 0, j)),
        ],
        out_specs=pl.BlockSpec((1, 8, tn), lambda l, j: (l, 0, j)),
        out_shape=jax.ShapeDtypeStruct((depth, 8, n), F32),
        compiler_params=_cparams(("parallel", "parallel")),
        name="modulation",
    )(c8, w_mod, b_mod.reshape(depth, 1, n))


class _Stream:
    def __init__(self, n_batch, t_batch, tm, row_w, long_stride, mod_row):
        self.n_batch = n_batch
        self.t_batch = t_batch
        self.tm = tm
        self.row_w = row_w
        self.long_stride = long_stride
        self.mod_row = mod_row
        self.tpb = t_batch // tm
        self.n_tiles = n_batch * self.tpb
        self.n_tok = n_batch * t_batch

    def mod_spec(self, ngrid):
        if self.mod_row is None:
            tpb = self.tpb
            if ngrid == 1:
                return pl.BlockSpec((1, 6, D_MODEL), lambda i: (i // tpb, 0, 0))
            return pl.BlockSpec((1, 6, D_MODEL), lambda i, j: (i // tpb, 0, 0))
        r = self.mod_row
        if ngrid == 1:
            return pl.BlockSpec((1, 6, D_MODEL), lambda i: (r, 0, 0))
        return pl.BlockSpec((1, 6, D_MODEL), lambda i, j: (r, 0, 0))


def _ev_kq_kernel(x_ref, mod_ref, wkq_ref, ckq_ref, wg_ref, bg_ref, kq_ref, g_ref, h_scr, *, row_w):
    j = pl.program_id(1)

    @pl.when(j == 0)
    def _():
        hb = _ada(x_ref[...], mod_ref, 0).astype(BF16)
        h_scr[...] = hb
        g = _dot(hb, wg_ref[...]) + bg_ref[...]
        lane = lax.broadcasted_iota(jnp.int32, g.shape, 1) % LANES
        is_forget = (lane < 4) & (lane % 2 == 1)
        g_ref[...] = jnp.where(is_forget, _log_sigmoid(g), g)

    a = _silu(_conv3(_dot(h_scr[...], wkq_ref[...]), ckq_ref[...], row_w))
    kq_ref[...] = a * jnp.where(j == 0, ML_DK ** -0.5, 1.0)


def _ev_kq(st, x, mod, wkq, ckq, wg, bg):
    tm = st.tm
    return pl.pallas_call(
        functools.partial(_ev_kq_kernel, row_w=st.row_w),
        grid=(st.n_tiles, 2),
        in_specs=[
            pl.BlockSpec((tm, D_MODEL), lambda i, j: (i, 0)),
            st.mod_spec(2),
            pl.BlockSpec((D_MODEL, ML_KW), lambda i, j: (0, j)),
            pl.BlockSpec((3, ML_KW), lambda i, j: (0, j)),
            pl.BlockSpec((D_MODEL, ML_HEADS * LANES), lambda i, j: (0, 0)),
            pl.BlockSpec((1, ML_HEADS * LANES), lambda i, j: (0, 0)),
        ],
        out_specs=[
            pl.BlockSpec((tm, ML_KW), lambda i, j: (i, j)),
            pl.BlockSpec((tm, ML_HEADS * LANES), lambda i, j: (i, 0)),
        ],
        out_shape=[
            jax.ShapeDtypeStruct((st.n_tok, 2 * ML_KW), F32),
            jax.ShapeDtypeStruct((st.n_tok, ML_HEADS * LANES), F32),
        ],
        scratch_shapes=[pltpu.VMEM((tm, D_MODEL), BF16)],
        compiler_params=_cparams(("parallel", "arbitrary")),
        name="ev_kq",
    )(x, mod, wkq, ckq, wg, bg)


def _ev_vo_kernel(x_ref, mod_ref, w_ref, o_ref, h_scr, *, n_plain):
    j = pl.program_id(1)

    @pl.when(j == 0)
    def _():
        h_scr[...] = _ada(x_ref[...], mod_ref, 0).astype(BF16)

    p = _dot(h_scr[...], w_ref[...])

    @pl.when(j < n_plain)
    def _():
        o_ref[...] = p

    @pl.when(j >= n_plain)
    def _():
        o_ref[...] = jax.nn.sigmoid(p)


def _ev_vo(st, x, mod, wvo):
    tm, tn = st.tm, 512
    return pl.pallas_call(
        functools.partial(_ev_vo_kernel, n_plain=ML_VW // tn),
        grid=(st.n_tiles, 2 * ML_VW // tn),
        in_specs=[
            pl.BlockSpec((tm, D_MODEL), lambda i, j: (i, 0)),
            st.mod_spec(2),
            pl.BlockSpec((D_MODEL, tn), lambda i, j: (0, j)),
        ],
        out_specs=pl.BlockSpec((tm, tn), lambda i, j: (i, j)),
        out_shape=jax.ShapeDtypeStruct((st.n_tok, 2 * ML_VW), F32),
        scratch_shapes=[pltpu.VMEM((tm, D_MODEL), BF16)],
        compiler_params=_cparams(("parallel", "arbitrary")),
        name="ev_vo",
    )(x, mod, wvo)


def _ev_sc_kernel(x_ref, mod_ref, w_ref, c_ref, o_ref, h_scr, *, row_w):
    @pl.when(pl.program_id(1) == 0)
    def _():
        h_scr[...] = _ada(x_ref[...], mod_ref, 0).astype(BF16)

    h = h_scr[...]
    gate_c = _dot(h, w_ref[0])
    gate_b = _dot(h, w_ref[1])
    val = _dot(h, w_ref[2])
    o_ref[...] = gate_b * _conv3(gate_c * val, c_ref[...], row_w)


def _ev_sc(st, x, mod, wsc, csc):
    tm, tn = st.tm, 256
    return pl.pallas_call(
        functools.partial(_ev_sc_kernel, row_w=st.row_w),
        grid=(st.n_tiles, SC_W // tn),
        in_specs=[
            pl.BlockSpec((tm, D_MODEL), lambda i, j: (i, 0)),
            st.mod_spec(2),
            pl.BlockSpec((3, D_MODEL, tn), lambda i, j: (0, 0, j)),
            pl.BlockSpec((3, tn), lambda i, j: (0, j)),
        ],
        out_specs=pl.BlockSpec((tm, tn), lambda i, j: (i, j)),
        out_shape=jax.ShapeDtypeStruct((st.n_tok, SC_W), F32),
        scratch_shapes=[pltpu.VMEM((tm, D_MODEL), BF16)],
        compiler_params=_cparams(("parallel", "arbitrary")),
        name="ev_sc",
    )(x, mod, wsc, csc)


def _mlstm_kernel(k_ref, q_ref, v_ref, so_ref, g_ref, c0_ref, n0_ref, m0_ref, ng_ref,
                  y_ref, cf_ref, nf_ref, mf_ref, hacc, kt_scr, *, nch):
    L = ML_CHUNK
    r_i = lax.broadcasted_iota(jnp.int32, (L, L), 0)
    c_i = lax.broadcasted_iota(jnp.int32, (L, L), 1)

    def fill_kt(c, carry):
        r = pl.multiple_of(c * L, L)
        kt_scr[c] = k_ref[pl.ds(r, L), :].T.astype(BF16)
        return carry

    lax.fori_loop(0, nch, fill_kt, 0)

    for d in range(2):
        tri = (c_i <= r_i) if d == 0 else (c_i >= r_i)
        tri_f = tri.astype(F32)

        def body(it, carry, d=d, tri=tri, tri_f=tri_f):
            ct, n_prev, m_prev = carry
            c = it if d == 0 else nch - 1 - it
            r = pl.multiple_of(c * L, L)
            gch = g_ref[pl.ds(r, L), :]
            bch = jnp.dot(tri_f, gch, precision=lax.Precision.HIGHEST,
                          preferred_element_type=F32)
            gt = gch.T
            bt = bch.T
            ig_col, b_col = gch[:, 2 * d:2 * d + 1], bch[:, 2 * d + 1:2 * d + 2]
            ig_row, b_row = gt[2 * d:2 * d + 1, :], bt[2 * d + 1:2 * d + 2, :]
            b_last = b_col[L - 1:L, :] if d == 0 else b_col[0:1, :]
            qc = q_ref[pl.ds(r, L), :]
            kc = k_ref[pl.ds(r, L), :]
            vc = v_ref[pl.ds(r, L), :]
            qb = qc.astype(BF16)
            ktc = kt_scr[c]

            log_intra = jnp.where(tri, b_col - b_row + ig_row, -jnp.inf)
            log_inter = b_col + m_prev
            m_t = jnp.maximum(log_inter, jnp.max(log_intra, axis=-1, keepdims=True))
            s = _dot(qb, ktc) * jnp.exp(log_intra - m_t)
            w_inter = jnp.exp(log_inter - m_t)
            num = _dot(s.astype(BF16), vc.astype(BF16)) + w_inter * _dot(qb, ct.astype(BF16))
            den = jnp.sum(s, axis=-1, keepdims=True) + w_inter * jnp.sum(qc * n_prev, axis=-1, keepdims=True)
            hch = num / jnp.maximum(jnp.abs(den), jnp.exp(-m_t))
            if d == 0:
                hacc[pl.ds(r, L), :] = hch
            else:
                h = hacc[pl.ds(r, L), :] + hch
                mu = jnp.mean(h, axis=-1, keepdims=True)
                hc = h - mu
                var = jnp.mean(hc * hc, axis=-1, keepdims=True)
                y_ref[pl.ds(r, L), :] = hc * lax.rsqrt(var + LN_EPS) * ng_ref[...] * so_ref[pl.ds(r, L), :]

            a_row = b_last - b_row + ig_row
            a_col = b_last - b_col + ig_col
            m_loc = jnp.max(a_row, axis=-1, keepdims=True)
            w_col = jnp.exp(a_col - m_loc)
            ct_loc = _dot(ktc, (w_col * vc).astype(BF16))
            n_loc = jnp.sum(w_col * kc, axis=0, keepdims=True)
            m_new = jnp.maximum(b_last + m_prev, m_loc)
            d_prev = jnp.exp(b_last + m_prev - m_new)
            d_loc = jnp.exp(m_loc - m_new)
            return d_prev * ct + d_loc * ct_loc, d_prev * n_prev + d_loc * n_loc, m_new

        init = (c0_ref[0, 0, d], n0_ref[0, 0, d], m0_ref[0, 0, d][:, 0:1])
        ct, n_fin, m_fin = lax.fori_loop(0, nch, body, init)
        cf_ref[0, 0, d] = ct
        nf_ref[0, 0, d] = n_fin
        mf_ref[0, 0, d] = jnp.broadcast_to(m_fin, (1, LANES))


def _mlstm(st, kq, vo, gates, state0, norm_g):
    tb = st.t_batch
    nb, nh = st.n_batch, ML_HEADS
    c0, n0, m0 = state0
    state_specs = [
        pl.BlockSpec((1, 1, 2, ML_DK, ML_DV), lambda b, h: (b, h, 0, 0, 0)),
        pl.BlockSpec((1, 1, 2, 1, ML_DK), lambda b, h: (b, h, 0, 0, 0)),
        pl.BlockSpec((1, 1, 2, 1, LANES), lambda b, h: (b, h, 0, 0, 0)),
    ]
    y, cf, nf, mf = pl.pallas_call(
        functools.partial(_mlstm_kernel, nch=tb // ML_CHUNK),
        grid=(nb, nh),
        in_specs=[
            pl.BlockSpec((tb, ML_DK), lambda b, h: (b, h)),
            pl.BlockSpec((tb, ML_DK), lambda b, h: (b, ML_HEADS + h)),
            pl.BlockSpec((tb, ML_DV), lambda b, h: (b, h)),
            pl.BlockSpec((tb, ML_DV), lambda b, h: (b, ML_HEADS + h)),
            pl.BlockSpec((tb, LANES), lambda b, h: (b, h)),
        ] + state_specs + [pl.BlockSpec((1, ML_DV), lambda b, h: (0, h))],
        out_specs=[pl.BlockSpec((tb, ML_DV), lambda b, h: (b, h))] + state_specs,
        out_shape=[
            jax.ShapeDtypeStruct((st.n_tok, ML_VW), F32),
            jax.ShapeDtypeStruct((nb, nh, 2, ML_DK, ML_DV), F32),
            jax.ShapeDtypeStruct((nb, nh, 2, 1, ML_DK), F32),
            jax.ShapeDtypeStruct((nb, nh, 2, 1, LANES), F32),
        ],
        scratch_shapes=[pltpu.VMEM((tb, ML_DV), F32), pltpu.VMEM((tb // ML_CHUNK, ML_DK, ML_CHUNK), BF16)],
        compiler_params=_cparams(("parallel", "parallel")),
        name="mlstm",
    )(kq, kq, vo, vo, gates, c0, n0, m0, norm_g)
    return y, (cf, nf, mf)


def _residual_ln(x, gate, y, g_ref, b_ref, alpha):
    return _ln(alpha * x + gate * y, g_ref[...], b_ref[...])


def _ev_out_kernel(yml_ref, ysc_ref, x_ref, mod_ref, w_ref, g_ref, b_ref, o_ref, *, alpha):
    y = _dot(yml_ref[...].astype(BF16), w_ref[0:ML_VW, :]) + _dot(ysc_ref[...].astype(BF16), w_ref[ML_VW:, :])
    o_ref[...] = _residual_ln(x_ref[...], mod_ref[0, 2:3, :], y, g_ref, b_ref, alpha)


def _ev_out(st, yml, ysc, x, mod, w_out, ln_g, ln_b, alpha):
    tm = st.tm
    tok = lambda w: pl.BlockSpec((tm, w), lambda i: (i, 0))
    full = lambda a, b: pl.BlockSpec((a, b), lambda i: (0, 0))
    return pl.pallas_call(
        functools.partial(_ev_out_kernel, alpha=alpha),
        grid=(st.n_tiles,),
        in_specs=[tok(ML_VW), tok(SC_W), tok(D_MODEL), st.mod_spec(1), full(2 * ML_VW, D_MODEL),
                  full(1, D_MODEL), full(1, D_MODEL)],
        out_specs=tok(D_MODEL),
        out_shape=jax.ShapeDtypeStruct((st.n_tok, D_MODEL), F32),
        compiler_params=_cparams(("parallel",)),
        name="ev_out",
    )(yml, ysc, x, mod, w_out, ln_g, ln_b)


def _ffn_kernel(x_ref, mod_ref, wup_ref, cw_ref, wdn_ref, g_ref, b_ref, o_ref, h_scr, acc, *, row_w, alpha):
    j = pl.program_id(1)

    @pl.when(j == 0)
    def _():
        h_scr[...] = _ada(x_ref[...], mod_ref, 3).astype(BF16)
        acc[...] = jnp.zeros_like(acc)

    h = h_scr[...]
    a = _conv3(_dot(h, wup_ref[0]), cw_ref[0], row_w)
    g = _conv3(_dot(h, wup_ref[1]), cw_ref[1], row_w)
    acc[...] += _dot((_silu(g) * a).astype(BF16), wdn_ref[...])

    @pl.when(j == pl.num_programs(1) - 1)
    def _():
        o_ref[...] = _residual_ln(x_ref[...], mod_ref[0, 5:6, :], acc[...], g_ref, b_ref, alpha)


def _ffn(st, x, mod, w_up, c_up, w_down, ln_g, ln_b, alpha):
    tm, fc = st.tm, FFN_CHUNK
    return pl.pallas_call(
        functools.partial(_ffn_kernel, row_w=st.row_w, alpha=alpha),
        grid=(st.n_tiles, D_FF // fc),
        in_specs=[
            pl.BlockSpec((tm, D_MODEL), lambda i, j: (i, 0)),
            st.mod_spec(2),
            pl.BlockSpec((2, D_MODEL, fc), lambda i, j: (0, 0, j)),
            pl.BlockSpec((2, 3, fc), lambda i, j: (0, 0, j)),
            pl.BlockSpec((fc, D_MODEL), lambda i, j: (j, 0)),
            pl.BlockSpec((1, D_MODEL), lambda i, j: (0, 0)),
            pl.BlockSpec((1, D_MODEL), lambda i, j: (0, 0)),
        ],
        out_specs=pl.BlockSpec((tm, D_MODEL), lambda i, j: (i, 0)),
        out_shape=jax.ShapeDtypeStruct((st.n_tok, D_MODEL), F32),
        scratch_shapes=[pltpu.VMEM((tm, D_MODEL), BF16), pltpu.VMEM((tm, D_MODEL), F32)],
        compiler_params=_cparams(("parallel", "arbitrary")),
        name="ffn",
    )(x, mod, w_up, c_up, w_down, ln_g, ln_b)


def _od_in_kernel(x_ref, mod_ref, w_ref, gu_ref, gv_ref, glu_ref, h_scr):
    @pl.when(pl.program_id(1) == 0)
    def _():
        h_scr[...] = _ada(x_ref[...], mod_ref, 0).astype(BF16)

    h = h_scr[...]
    gu_ref[...] = _gelu(_dot(h, w_ref[0]))
    gv_ref[...] = _gelu(_dot(h, w_ref[1]))
    glu_ref[...] = _dot(h, w_ref[2]) * jax.nn.sigmoid(_dot(h, w_ref[3]))


def _od_in(st, x, mod, w4):
    tm, tn = st.tm, 256
    out = jax.ShapeDtypeStruct((st.n_tok, GM_W), F32)
    ospec = pl.BlockSpec((tm, tn), lambda i, j: (i, j))
    return pl.pallas_call(
        _od_in_kernel,
        grid=(st.n_tiles, GM_W // tn),
        in_specs=[
            pl.BlockSpec((tm, D_MODEL), lambda i, j: (i, 0)),
            st.mod_spec(2),
            pl.BlockSpec((4, D_MODEL, tn), lambda i, j: (0, 0, j)),
        ],
        out_specs=[ospec, ospec, ospec],
        out_shape=[out, out, out],
        scratch_shapes=[pltpu.VMEM((tm, D_MODEL), BF16)],
        compiler_params=_cparams(("parallel", "arbitrary")),
        name="od_in",
    )(x, mod, w4)


def _conv_long_kernel(x_ref, w_ref, o_ref, pad, *, stride, tb):
    half = (CF_CONV // 2) * stride
    cb = x_ref.shape[1]
    pad[0:half, :] = jnp.zeros((half, cb), F32)
    pad[half + tb:, :] = jnp.zeros((half, cb), F32)
    pad[half:half + tb, :] = x_ref[...]
    w = w_ref[...]

    def block(r0):
        acc = jnp.zeros((CONV_ROWS, cb), F32)
        for j in range(CF_CONV):
            acc = acc + w[j:j + 1, :] * pad[pl.ds(r0 + j * stride, CONV_ROWS), :]
        o_ref[pl.ds(r0, CONV_ROWS), :] = acc

    n_blocks = tb // CONV_ROWS
    if stride % 8 == 0:
        def body(i, carry):
            block(pl.multiple_of(i * CONV_ROWS, CONV_ROWS))
            return carry
        lax.fori_loop(0, n_blocks, body, 0)
    else:
        for i in range(n_blocks):
            block(i * CONV_ROWS)


def _conv_long(st, x, w):
    tb, cb = st.t_batch, 256
    stride = st.long_stride
    return pl.pallas_call(
        functools.partial(_conv_long_kernel, stride=stride, tb=tb),
        grid=(st.n_batch, CF_W // cb),
        in_specs=[
            pl.BlockSpec((tb, cb), lambda b, j: (b, j)),
            pl.BlockSpec((CF_CONV, cb), lambda b, j: (0, j)),
        ],
        out_specs=pl.BlockSpec((tb, cb), lambda b, j: (b, j)),
        out_shape=jax.ShapeDtypeStruct((st.n_tok, CF_W), F32),
        scratch_shapes=[pltpu.VMEM((tb + 2 * (CF_CONV // 2) * stride, cb), F32)],
        compiler_params=_cparams(("parallel", "parallel")),
        name="conv_long",
    )(x, w)


def _od_out_kernel(gu_ref, gv_ref, cv_ref, x_ref, mod_ref, sg_ref, sb_ref, ws_ref, bs_ref, cg_ref, cb_ref,
                   w_ref, g_ref, b_ref, o_ref, y_scr, *, alpha):
    tm = gu_ref.shape[0]
    gw = GM_W // GM_GROUPS
    vn = _ln(gv_ref[...], sg_ref[...], sb_ref[...])
    for ci in range(tm // GM_CHUNK):
        rows = slice(ci * GM_CHUNK, (ci + 1) * GM_CHUNK)
        for g in range(GM_GROUPS):
            cols = slice(g * gw, (g + 1) * gw)
            mixed = _dot(ws_ref[g], vn[rows, cols].astype(BF16)) + bs_ref[:, g:g + 1]
            y_scr[rows, cols] = (gu_ref[rows, cols] * mixed).astype(BF16)
    y_scr[:, GM_W:] = _silu(_ln(cv_ref[...], cg_ref[...], cb_ref[...])).astype(BF16)
    y = _dot(y_scr[...], w_ref[...])
    o_ref[...] = _residual_ln(x_ref[...], mod_ref[0, 2:3, :], y, g_ref, b_ref, alpha)


def _od_out(st, gu, gv, cv, x, mod, sgu_g, sgu_b, w_s, b_s_t, cf_g, cf_b, w_out, ln_g, ln_b, alpha):
    tm = st.tm
    tok = lambda w: pl.BlockSpec((tm, w), lambda i: (i, 0))
    full = lambda a, b: pl.BlockSpec((a, b), lambda i: (0, 0))
    return pl.pallas_call(
        functools.partial(_od_out_kernel, alpha=alpha),
        grid=(st.n_tiles,),
        in_specs=[tok(GM_W), tok(GM_W), tok(CF_W), tok(D_MODEL), st.mod_spec(1),
                  full(1, GM_W), full(1, GM_W),
                  pl.BlockSpec((GM_GROUPS, GM_CHUNK, GM_CHUNK), lambda i: (0, 0, 0)),
                  full(GM_CHUNK, GM_GROUPS), full(1, CF_W), full(1, CF_W),
                  full(GM_W + CF_W, D_MODEL), full(1, D_MODEL), full(1, D_MODEL)],
        out_specs=tok(D_MODEL),
        out_shape=jax.ShapeDtypeStruct((st.n_tok, D_MODEL), F32),
        scratch_shapes=[pltpu.VMEM((tm, GM_W + CF_W), BF16)],
        compiler_params=_cparams(("parallel",)),
        name="od_out",
    )(gu, gv, cv, x, mod, sgu_g, sgu_b, w_s, b_s_t, cf_g, cf_b, w_out, ln_g, ln_b)


def _prep_even(w_in, b_gates):
    wkq = jnp.concatenate([w_in[:, 0:ML_KW], w_in[:, EV_Q0:EV_O0]], axis=1).astype(BF16)
    wvo = jnp.concatenate([w_in[:, EV_V0:EV_G0], w_in[:, EV_O0:EV_SC0]], axis=1).astype(BF16)
    wsc = jnp.stack([w_in[:, EV_SC0 + i * SC_W:EV_SC0 + (i + 1) * SC_W] for i in range(3)]).astype(BF16)

    def per_head(g):
        lead = g.shape[:-1]
        g = g.reshape(lead + (2, 2, ML_HEADS))
        g = jnp.moveaxis(g, -1, -3).reshape(lead + (ML_HEADS, 4))
        g = jnp.pad(g, [(0, 0)] * (len(lead) + 1) + [(0, LANES - 4)])
        return g.reshape(lead + (ML_HEADS * LANES,))

    wg = per_head(w_in[:, EV_G0:EV_G0 + ML_GATES]).astype(BF16)
    bg = per_head(b_gates.reshape(1, ML_GATES))
    return wkq, wvo, wsc, wg, bg


def _zero_state(nb):
    return (jnp.zeros((nb, ML_HEADS, 2, ML_DK, ML_DV), F32),
            jnp.zeros((nb, ML_HEADS, 2, 1, ML_DK), F32),
            jnp.full((nb, ML_HEADS, 2, 1, LANES), M_INIT, F32))


def kernel(x, c, ctx, c_ctx, w_mod, b_mod, ln1_g, ln1_b, ln2_g, ln2_b, ffn_w_up, ffn_conv, ffn_w_down,
           ev_w_in, ev_b_gates, ev_conv_k, ev_conv_q, ev_norm_g, ev_conv_sc, ev_w_out,
           od_w_in, od_sgu_g, od_sgu_b, od_w_s, od_b_s, od_conv_dw, od_cf_g, od_cf_b, od_w_out):
    nb, seq, d = x.shape
    ctx_len = ctx.shape[1]
    depth = w_mod.shape[0]
    assert d == D_MODEL and seq % TM_LATENT == 0 and ctx_len % ML_CHUNK == 0 and nb + 1 <= 8
    alpha = (2.0 * depth) ** 0.25
    last_even = ((depth - 1) // 2) * 2

    lat = _Stream(nb, seq, TM_LATENT, GRID_W, GRID_W, None)
    cst = _Stream(nb, ctx_len, ctx_len, ctx_len, 1, nb)

    c8 = jnp.zeros((8, d), F32).at[:nb].set(c).at[nb].set(c_ctx)
    mod_all = _modulation(c8, w_mod, b_mod).reshape(depth, 8, 6, d)

    xs = x.reshape(nb * seq, d)
    cs = ctx.reshape(nb * ctx_len, d)
    row = lambda v: v.reshape(1, -1)

    for l in range(depth):
        ctx_out = l < last_even
        j = l // 2
        mod = mod_all[l]
        w_up = ffn_w_up[l].reshape(d, 2, D_FF).transpose(1, 0, 2).astype(BF16)
        c_up = ffn_conv[l].reshape(3, 2, D_FF).transpose(1, 0, 2)
        w_down = ffn_w_down[l].astype(BF16)
        g1, b1, g2, b2 = row(ln1_g[l]), row(ln1_b[l]), row(ln2_g[l]), row(ln2_b[l])

        if l % 2 == 0:
            wkq, wvo, wsc, wg, bg = _prep_even(ev_w_in[j], ev_b_gates[j])
            ckq = jnp.concatenate([ev_conv_k[j], ev_conv_q[j]], axis=1)
            w_out = ev_w_out[j].astype(BF16)
            norm_g = row(ev_norm_g[j])

            def even_mixer(st, z, state0, want_y):
                kq, gates = _ev_kq(st, z, mod, wkq, ckq, wg, bg)
                vo = _ev_vo(st, z, mod, wvo)
                yml, fin = _mlstm(st, kq, vo, gates, state0, norm_g)
                if not want_y:
                    return None, fin
                ysc = _ev_sc(st, z, mod, wsc, ev_conv_sc[j])
                return _ev_out(st, yml, ysc, z, mod, w_out, g1, b1, alpha), fin

            c1, fin = even_mixer(cst, cs, _zero_state(nb), ctx_out)
            x1, _ = even_mixer(lat, xs, fin, True)
        else:
            w4 = od_w_in[j].reshape(d, 4, GM_W).transpose(1, 0, 2).astype(BF16)
            w_s = od_w_s[j].astype(BF16)
            b_s_t = od_b_s[j].T
            w_out = od_w_out[j].astype(BF16)

            def odd_mixer(st, z):
                gu, gv, glu = _od_in(st, z, mod, w4)
                cv = _conv_long(st, glu, od_conv_dw[j])
                return _od_out(st, gu, gv, cv, z, mod, row(od_sgu_g[j]), row(od_sgu_b[j]), w_s, b_s_t,
                               row(od_cf_g[j]), row(od_cf_b[j]), w_out, g1, b1, alpha)

            x1 = odd_mixer(lat, xs)
            c1 = odd_mixer(cst, cs) if ctx_out else None

        xs = _ffn(lat, x1, mod, w_up, c_up, w_down, g2, b2, alpha)
        if ctx_out:
            cs = _ffn(cst, c1, mod, w_up, c_up, w_down, g2, b2, alpha)

    return xs.reshape(nb, seq, d)
```

```python
import functools

import jax
import jax.numpy as jnp
from jax import lax
from jax.experimental import pallas as pl
from jax.experimental.pallas import tpu as pltpu

F32 = jnp.float32
BF16 = jnp.bfloat16

D_MODEL = 2048
GRID_W = 64
ML_HEADS = 4
ML_DV = 256
ML_DK = 128
ML_VW = ML_HEADS * ML_DV
ML_KW = ML_HEADS * ML_DK
ML_GATES = 16
ML_CHUNK = 128
M_INIT = -1e30
SC_W = 1024
GM_W = 1024
GM_GROUPS = 4
GM_CHUNK = 128
CF_W = 1024
CF_CONV = 31
D_FF = 5632
LN_EPS = 1e-5

EV_V0 = ML_KW
EV_G0 = EV_V0 + ML_VW
EV_Q0 = EV_G0 + ML_GATES
EV_O0 = EV_Q0 + ML_KW
EV_SC0 = EV_O0 + ML_VW

V7X_VMEM_BYTES = 64 * 2**20
VMEM_LIMIT = V7X_VMEM_BYTES - 8 * 2**20
LANES = 128
BF16_ROWS = 16

TM_LATENT = 512
TM_IN = 256
FFN_CHUNK = 512
CONV_ROWS = 64
ML_DVX = ML_DV + BF16_ROWS


def _cparams(sem):
    return pltpu.CompilerParams(dimension_semantics=sem, vmem_limit_bytes=VMEM_LIMIT)


def _silu(z):
    return z * jax.nn.sigmoid(z)


def _gelu(z):
    return 0.5 * z * (1.0 + lax.erf(z * (2.0 ** -0.5)))


def _log_sigmoid(z):
    return jnp.minimum(z, 0.0) - jnp.log1p(jnp.exp(-jnp.abs(z)))


def _ln(z, g, b):
    mu = jnp.mean(z, axis=-1, keepdims=True)
    zc = z - mu
    var = jnp.mean(zc * zc, axis=-1, keepdims=True)
    return zc * lax.rsqrt(var + LN_EPS) * g + b


def _ada(x, mod_ref, slot):
    return x * (1.0 + mod_ref[0, slot + 1:slot + 2, :]) + mod_ref[0, slot:slot + 1, :]


def _conv3(a, w, row_w):
    tm = a.shape[0]
    t = lax.broadcasted_iota(jnp.int32, (tm, 1), 0) % row_w
    prev = jnp.where(t == 0, 0.0, pltpu.roll(a, 1, 0))
    nxt = jnp.where(t == row_w - 1, 0.0, pltpu.roll(a, tm - 1, 0))
    return w[0:1, :] * prev + w[1:2, :] * a + w[2:3, :] * nxt


def _dot(a, b):
    return jnp.dot(a, b, preferred_element_type=F32)


def _tri_sum(tri, z):
    hi = z.astype(BF16)
    r1 = z - hi.astype(F32)
    mid = r1.astype(BF16)
    lo = (r1 - mid.astype(F32)).astype(BF16)
    return _dot(tri, hi) + _dot(tri, mid) + _dot(tri, lo)


def _resident(shape):
    zeros = (0,) * len(shape)
    return pl.BlockSpec(shape, lambda *_: zeros, pipeline_mode=pl.Buffered(1))


def _mod_kernel(c_ref, w_ref, b_ref, o_ref):
    s = _silu(c_ref[...]).astype(BF16)
    o_ref[0] = _dot(s, w_ref[0].astype(BF16)) + b_ref[0]


def _modulation(c8, w_mod, b_mod):
    depth, d, n = w_mod.shape
    tn = 1024
    return pl.pallas_call(
        _mod_kernel,
        grid=(depth, n // tn),
        in_specs=[
            pl.BlockSpec((8, d), lambda l, j: (0, 0)),
            pl.BlockSpec((1, d, tn), lambda l, j: (l, 0, j)),
            pl.BlockSpec((1, 1, tn), lambda l, j: (l, 0, j)),
        ],
        out_specs=pl.BlockSpec((1, 8, tn), lambda l, j: (l, 0, j)),
        out_shape=jax.ShapeDtypeStruct((depth, 8, n), F32),
        compiler_params=_cparams(("parallel", "parallel")),
        name="modulation",
    )(c8, w_mod, b_mod.reshape(depth, 1, n))


class _Stream:
    def __init__(self, n_batch, t_batch, tm, tm_in, row_w, long_stride, mod_row):
        self.n_batch = n_batch
        self.t_batch = t_batch
        self.tm = tm
        self.tm_in = tm_in
        self.row_w = row_w
        self.long_stride = long_stride
        self.mod_row = mod_row
        self.n_tok = n_batch * t_batch

    def mod_spec(self, tm):
        if self.mod_row is None:
            tpb = self.t_batch // tm
            return pl.BlockSpec((1, 6, D_MODEL), lambda i, *_: (i // tpb, 0, 0))
        r = self.mod_row
        return pl.BlockSpec((1, 6, D_MODEL), lambda i, *_: (r, 0, 0))


def _ev_in_kernel(x_ref, mod_ref, wk_ref, wq_ref, wv_ref, wo_ref, wsc_ref, wg_ref, bg_ref, ck_ref, cq_ref, csc_ref,
                  k_ref, qt_ref, vt_ref, so_ref, ysc_ref, g_ref, gt_ref, *, row_w):
    L = ML_CHUNK
    tm = x_ref.shape[0]
    chunks = [slice(ci * L, (ci + 1) * L) for ci in range(tm // L)]
    hb = _ada(x_ref[...], mod_ref, 0).astype(BF16)

    k = _silu(_conv3(_dot(hb, wk_ref[...]), ck_ref[...], row_w)) * (ML_DK ** -0.5)
    k_ref[...] = k.astype(BF16)

    q = _silu(_conv3(_dot(hb, wq_ref[...]), cq_ref[...], row_w))
    for ci, rows in enumerate(chunks):
        for f in range(ML_KW // L):
            qt_ref[ci, f * L:(f + 1) * L, :] = q[rows, f * L:(f + 1) * L].T.astype(BF16)

    v = _dot(hb, wv_ref[...])
    for ci, rows in enumerate(chunks):
        for f in range(ML_VW // L):
            vt_ref[ci, f * L:(f + 1) * L, :] = v[rows, f * L:(f + 1) * L].T.astype(BF16)

    so_ref[...] = jax.nn.sigmoid(_dot(hb, wo_ref[...])).astype(BF16)

    gate_c = _dot(hb, wsc_ref[0])
    val = _dot(hb, wsc_ref[2])
    mixed = _conv3(gate_c * val, csc_ref[...], row_w)
    ysc_ref[...] = (_dot(hb, wsc_ref[1]) * mixed).astype(BF16)

    g = _dot(hb, wg_ref[...]) + bg_ref[...]
    lane = lax.broadcasted_iota(jnp.int32, (L, g.shape[1]), 1) % LANES
    r_i = lax.broadcasted_iota(jnp.int32, (L, L), 0)
    c_i = lax.broadcasted_iota(jnp.int32, (L, L), 1)
    lower = (c_i <= r_i).astype(BF16)
    upper = (c_i >= r_i).astype(BF16)
    for ci, rows in enumerate(chunks):
        gc = g[rows, :]
        lf = _log_sigmoid(gc)
        out = jnp.where(lane == 1, _tri_sum(lower, lf), jnp.where(lane == 3, _tri_sum(upper, lf), gc))
        g_ref[rows, :] = out
        for hd in range(ML_HEADS):
            gt_ref[ci, hd] = out[:, hd * LANES:(hd + 1) * LANES].T[0:8, :]


def _ev_in(st, x, mod, wts):
    wk, wq, wv, wo, wsc, wg, bg, ck, cq, csc = wts
    tm = st.tm_in
    nck = tm // ML_CHUNK
    n_chunks = st.n_tok // ML_CHUNK
    tok = lambda w: pl.BlockSpec((tm, w), lambda i: (i, 0))
    return pl.pallas_call(
        functools.partial(_ev_in_kernel, row_w=st.row_w),
        grid=(st.n_tok // tm,),
        in_specs=[tok(D_MODEL), st.mod_spec(tm)] + [_resident(w.shape) for w in wts],
        out_specs=[
            tok(ML_KW),
            pl.BlockSpec((nck, ML_KW, ML_CHUNK), lambda i: (i, 0, 0)),
            pl.BlockSpec((nck, ML_VW, ML_CHUNK), lambda i: (i, 0, 0)),
            tok(ML_VW),
            tok(SC_W),
            tok(ML_HEADS * LANES),
            pl.BlockSpec((nck, ML_HEADS, 8, ML_CHUNK), lambda i: (i, 0, 0, 0)),
        ],
        out_shape=[
            jax.ShapeDtypeStruct((st.n_tok, ML_KW), BF16),
            jax.ShapeDtypeStruct((n_chunks, ML_KW, ML_CHUNK), BF16),
            jax.ShapeDtypeStruct((n_chunks, ML_VW, ML_CHUNK), BF16),
            jax.ShapeDtypeStruct((st.n_tok, ML_VW), BF16),
            jax.ShapeDtypeStruct((st.n_tok, SC_W), BF16),
            jax.ShapeDtypeStruct((st.n_tok, ML_HEADS * LANES), F32),
            jax.ShapeDtypeStruct((n_chunks, ML_HEADS, 8, ML_CHUNK), F32),
        ],
        compiler_params=_cparams(("parallel",)),
        name="ev_in",
    )(x, mod, *wts)


def _mlstm_kernel(k_ref, qt_ref, vt_ref, so_ref, g_ref, gt_ref, c0_ref, m0_ref, ng_ref,
                  y_ref, cf_ref, mf_ref, st_scr, nm_scr, *, nch):
    L = ML_CHUNK
    s_i = lax.broadcasted_iota(jnp.int32, (L, L), 0)
    t_i = lax.broadcasted_iota(jnp.int32, (L, L), 1)
    row_8 = lax.broadcasted_iota(jnp.int32, (8, LANES), 0)

    def gate_rows(g8, d):
        ig_row, b_row = g8[2 * d:2 * d + 1, :], g8[2 * d + 1:2 * d + 2, :]
        b_last = b_row[:, L - 1:L] if d == 0 else b_row[:, 0:1]
        return ig_row, b_row, b_last

    def local_pass(c, carry):
        r = pl.multiple_of(c * L, L)
        g8 = gt_ref[c, 0]
        kc = k_ref[pl.ds(r, L), :]
        vtc = vt_ref[c].astype(F32)
        for d in range(2):
            ig_row, b_row, b_last = gate_rows(g8, d)
            a_row = b_last - b_row + ig_row
            m_loc = jnp.max(a_row, axis=-1, keepdims=True)
            w_row = jnp.exp(a_row - m_loc)
            lhs = jnp.concatenate([vtc * w_row, jnp.broadcast_to(w_row, (BF16_ROWS, L))], axis=0)
            st_scr[d, c] = _dot(lhs.astype(BF16), kc)
            nm_scr[d, c] = jnp.where(row_8 == 0, m_loc, b_last)
        return carry

    lax.fori_loop(0, nch, local_pass, 0, unroll=2)

    for d in range(2):
        def scan_pass(it, carry, d=d):
            cx, m_prev = carry
            c = it if d == 0 else nch - 1 - it
            loc = st_scr[d, c]
            blk = nm_scr[d, c]
            m_loc, b_last = blk[0:1, 0:1], blk[1:2, 0:1]
            st_scr[d, c] = cx
            nm_scr[d, c] = jnp.broadcast_to(m_prev, (8, LANES))
            m_new = jnp.maximum(b_last + m_prev, m_loc)
            return jnp.exp(b_last + m_prev - m_new) * cx + jnp.exp(m_loc - m_new) * loc, m_new

        cx, m_fin = lax.fori_loop(0, nch, scan_pass, (c0_ref[0, 0, d], m0_ref[0, 0, d][:, 0:1]))
        cf_ref[0, 0, d] = cx
        mf_ref[0, 0, d] = jnp.broadcast_to(m_fin, (1, LANES))

    def output_pass(c, carry):
        r = pl.multiple_of(c * L, L)
        g8 = gt_ref[c, 0]
        gch = g_ref[pl.ds(r, L), :]
        qtc = qt_ref[c]
        vtc = vt_ref[c]
        sraw = _dot(k_ref[pl.ds(r, L), :], qtc)
        h = None
        for d in range(2):
            ig_row, b_row, _ = gate_rows(g8, d)
            src_col = gch[:, 2 * d:2 * d + 1] - gch[:, 2 * d + 1:2 * d + 2]
            visible = (s_i <= t_i) if d == 0 else (s_i >= t_i)
            log_intra = jnp.where(visible, src_col + b_row, -jnp.inf)
            m_prev = nm_scr[d, c][0:1, 0:1]
            log_inter = b_row + m_prev
            m_t = jnp.maximum(log_inter, jnp.max(log_intra, axis=0, keepdims=True))
            s = sraw * jnp.exp(log_intra - m_t)
            w_inter = jnp.exp(log_inter - m_t)
            inter = _dot(st_scr[d, c].astype(BF16), qtc)
            num = _dot(vtc, s.astype(BF16)) + w_inter * inter[0:ML_DV, :]
            den = jnp.sum(s, axis=0, keepdims=True) + w_inter * inter[ML_DV:ML_DV + 1, :]
            hd = num * (1.0 / jnp.maximum(jnp.abs(den), jnp.exp(-m_t)))
            h = hd if h is None else h + hd
        mu = jnp.mean(h, axis=0, keepdims=True)
        hc = h - mu
        var = jnp.mean(hc * hc, axis=0, keepdims=True)
        hn = (hc * lax.rsqrt(var + LN_EPS)).T
        y_ref[pl.ds(r, L), :] = (hn * ng_ref[...] * so_ref[pl.ds(r, L), :].astype(F32)).astype(BF16)
        return carry

    lax.fori_loop(0, nch, output_pass, 0, unroll=2)


def _mlstm(st, k, qt, vt, so, g, gt, state0, norm_g):
    tb = st.t_batch
    nch = tb // ML_CHUNK
    nb, nh = st.n_batch, ML_HEADS
    c0, m0 = state0
    state_specs = [
        pl.BlockSpec((1, 1, 2, ML_DVX, ML_DK), lambda b, h: (b, h, 0, 0, 0)),
        pl.BlockSpec((1, 1, 2, 1, LANES), lambda b, h: (b, h, 0, 0, 0)),
    ]
    y, cf, mf = pl.pallas_call(
        functools.partial(_mlstm_kernel, nch=nch),
        grid=(nb, nh),
        in_specs=[
            pl.BlockSpec((tb, ML_DK), lambda b, h: (b, h)),
            pl.BlockSpec((nch, ML_DK, ML_CHUNK), lambda b, h: (b, h, 0)),
            pl.BlockSpec((nch, ML_DV, ML_CHUNK), lambda b, h: (b, h, 0)),
            pl.BlockSpec((tb, ML_DV), lambda b, h: (b, h)),
            pl.BlockSpec((tb, LANES), lambda b, h: (b, h)),
            pl.BlockSpec((nch, 1, 8, ML_CHUNK), lambda b, h: (b, h, 0, 0)),
        ] + state_specs + [pl.BlockSpec((1, ML_DV), lambda b, h: (0, h))],
        out_specs=[pl.BlockSpec((tb, ML_DV), lambda b, h: (b, h))] + state_specs,
        out_shape=[
            jax.ShapeDtypeStruct((st.n_tok, ML_VW), BF16),
            jax.ShapeDtypeStruct((nb, nh, 2, ML_DVX, ML_DK), F32),
            jax.ShapeDtypeStruct((nb, nh, 2, 1, LANES), F32),
        ],
        scratch_shapes=[
            pltpu.VMEM((2, nch, ML_DVX, ML_DK), F32),
            pltpu.VMEM((2, nch, 8, LANES), F32),
        ],
        compiler_params=_cparams(("parallel", "parallel")),
        name="mlstm",
    )(k, qt, vt, so, g, gt, c0, m0, norm_g)
    return y, (cf, mf)


def _residual_ln(x, gate, y, g_ref, b_ref, alpha):
    return _ln(alpha * x + gate * y, g_ref[...], b_ref[...])


def _ev_out_kernel(yml_ref, ysc_ref, x_ref, mod_ref, w_ref, g_ref, b_ref, o_ref, *, alpha):
    y = _dot(yml_ref[...], w_ref[0:ML_VW, :]) + _dot(ysc_ref[...], w_ref[ML_VW:, :])
    o_ref[...] = _residual_ln(x_ref[...], mod_ref[0, 2:3, :], y, g_ref, b_ref, alpha)


def _ev_out(st, yml, ysc, x, mod, w_out, ln_g, ln_b, alpha):
    tm = st.tm
    tok = lambda w: pl.BlockSpec((tm, w), lambda i: (i, 0))
    return pl.pallas_call(
        functools.partial(_ev_out_kernel, alpha=alpha),
        grid=(st.n_tok // tm,),
        in_specs=[tok(ML_VW), tok(SC_W), tok(D_MODEL), st.mod_spec(tm), _resident(w_out.shape),
                  _resident(ln_g.shape), _resident(ln_b.shape)],
        out_specs=tok(D_MODEL),
        out_shape=jax.ShapeDtypeStruct((st.n_tok, D_MODEL), F32),
        compiler_params=_cparams(("parallel",)),
        name="ev_out",
    )(yml, ysc, x, mod, w_out, ln_g, ln_b)


def _ffn_kernel(x_ref, mod_ref, wup_ref, cw_ref, wdn_ref, g_ref, b_ref, o_ref, h_scr, acc, *, row_w, alpha):
    j = pl.program_id(1)

    @pl.when(j == 0)
    def _():
        h_scr[...] = _ada(x_ref[...], mod_ref, 3).astype(BF16)
        acc[...] = jnp.zeros_like(acc)

    h = h_scr[...]
    a = _conv3(_dot(h, wup_ref[0]), cw_ref[0], row_w)
    g = _conv3(_dot(h, wup_ref[1]), cw_ref[1], row_w)
    acc[...] += _dot((_silu(g) * a).astype(BF16), wdn_ref[...])

    @pl.when(j == pl.num_programs(1) - 1)
    def _():
        o_ref[...] = _residual_ln(x_ref[...], mod_ref[0, 5:6, :], acc[...], g_ref, b_ref, alpha)


def _ffn(st, x, mod, w_up, c_up, w_down, ln_g, ln_b, alpha):
    tm, fc = st.tm, FFN_CHUNK
    return pl.pallas_call(
        functools.partial(_ffn_kernel, row_w=st.row_w, alpha=alpha),
        grid=(st.n_tok // tm, D_FF // fc),
        in_specs=[
            pl.BlockSpec((tm, D_MODEL), lambda i, j: (i, 0)),
            st.mod_spec(tm),
            pl.BlockSpec((2, D_MODEL, fc), lambda i, j: (0, 0, j)),
            pl.BlockSpec((2, 3, fc), lambda i, j: (0, 0, j)),
            pl.BlockSpec((fc, D_MODEL), lambda i, j: (j, 0)),
            pl.BlockSpec((1, D_MODEL), lambda i, j: (0, 0)),
            pl.BlockSpec((1, D_MODEL), lambda i, j: (0, 0)),
        ],
        out_specs=pl.BlockSpec((tm, D_MODEL), lambda i, j: (i, 0)),
        out_shape=jax.ShapeDtypeStruct((st.n_tok, D_MODEL), F32),
        scratch_shapes=[pltpu.VMEM((tm, D_MODEL), BF16), pltpu.VMEM((tm, D_MODEL), F32)],
        compiler_params=_cparams(("parallel", "arbitrary")),
        name="ffn",
    )(x, mod, w_up, c_up, w_down, ln_g, ln_b)


def _od_in_kernel(x_ref, mod_ref, w_ref, sg_ref, sb_ref, ws_ref, bs_ref, ygm_ref, glu_ref):
    gw = GM_W // GM_GROUPS
    hb = _ada(x_ref[...], mod_ref, 0).astype(BF16)
    glu_ref[...] = (_dot(hb, w_ref[2]) * jax.nn.sigmoid(_dot(hb, w_ref[3]))).astype(BF16)
    vn = _ln(_gelu(_dot(hb, w_ref[1])), sg_ref[...], sb_ref[...]).astype(BF16)
    u = _gelu(_dot(hb, w_ref[0]))
    for ci in range(x_ref.shape[0] // GM_CHUNK):
        rows = slice(ci * GM_CHUNK, (ci + 1) * GM_CHUNK)
        for g in range(GM_GROUPS):
            cols = slice(g * gw, (g + 1) * gw)
            mixed = _dot(ws_ref[g], vn[rows, cols]) + bs_ref[:, g:g + 1]
            ygm_ref[rows, cols] = (u[rows, cols] * mixed).astype(BF16)


def _od_in(st, x, mod, w4, sgu_g, sgu_b, w_s, b_s_t):
    tm = st.tm_in
    tok = lambda w: pl.BlockSpec((tm, w), lambda i: (i, 0))
    out = jax.ShapeDtypeStruct((st.n_tok, GM_W), BF16)
    consts = (w4, sgu_g, sgu_b, w_s, b_s_t)
    return pl.pallas_call(
        _od_in_kernel,
        grid=(st.n_tok // tm,),
        in_specs=[tok(D_MODEL), st.mod_spec(tm)] + [_resident(a.shape) for a in consts],
        out_specs=[tok(GM_W), tok(CF_W)],
        out_shape=[out, out],
        compiler_params=_cparams(("parallel",)),
        name="od_in",
    )(x, mod, *consts)


def _conv_long_kernel(x_ref, w_ref, o_ref, pad, *, stride, tb):
    half = (CF_CONV // 2) * stride
    cb = x_ref.shape[1]
    pad[0:half, :] = jnp.zeros((half, cb), F32)
    pad[half + tb:, :] = jnp.zeros((half, cb), F32)
    pad[half:half + tb, :] = x_ref[...].astype(F32)
    w = w_ref[...]

    def block(r0, aligned):
        acc = jnp.zeros((CONV_ROWS, cb), F32)
        for j in range(CF_CONV):
            start = r0 + j * stride
            acc = acc + w[j:j + 1, :] * pad[pl.ds(pl.multiple_of(start, 8) if aligned else start, CONV_ROWS), :]
        o_ref[pl.ds(r0, CONV_ROWS), :] = acc.astype(o_ref.dtype)

    n_blocks = tb // CONV_ROWS
    if stride % 8 == 0:
        def body(i, carry):
            block(pl.multiple_of(i * CONV_ROWS, CONV_ROWS), True)
            return carry
        lax.fori_loop(0, n_blocks, body, 0)
    else:
        for i in range(n_blocks):
            block(i * CONV_ROWS, False)


def _conv_long(st, x, w):
    tb, cb = st.t_batch, 256
    stride = st.long_stride
    return pl.pallas_call(
        functools.partial(_conv_long_kernel, stride=stride, tb=tb),
        grid=(st.n_batch, CF_W // cb),
        in_specs=[
            pl.BlockSpec((tb, cb), lambda b, j: (b, j)),
            pl.BlockSpec((CF_CONV, cb), lambda b, j: (0, j)),
        ],
        out_specs=pl.BlockSpec((tb, cb), lambda b, j: (b, j)),
        out_shape=jax.ShapeDtypeStruct((st.n_tok, CF_W), BF16),
        scratch_shapes=[pltpu.VMEM((tb + 2 * (CF_CONV // 2) * stride, cb), F32)],
        compiler_params=_cparams(("parallel", "parallel")),
        name="conv_long",
    )(x, w)


def _od_out_kernel(ygm_ref, cv_ref, x_ref, mod_ref, cg_ref, cb_ref, w_ref, g_ref, b_ref, o_ref, *, alpha):
    ycf = _silu(_ln(cv_ref[...].astype(F32), cg_ref[...], cb_ref[...])).astype(BF16)
    y = _dot(ygm_ref[...], w_ref[0:GM_W, :]) + _dot(ycf, w_ref[GM_W:, :])
    o_ref[...] = _residual_ln(x_ref[...], mod_ref[0, 2:3, :], y, g_ref, b_ref, alpha)


def _od_out(st, ygm, cv, x, mod, cf_g, cf_b, w_out, ln_g, ln_b, alpha):
    tm = st.tm
    tok = lambda w: pl.BlockSpec((tm, w), lambda i: (i, 0))
    consts = (cf_g, cf_b, w_out, ln_g, ln_b)
    return pl.pallas_call(
        functools.partial(_od_out_kernel, alpha=alpha),
        grid=(st.n_tok // tm,),
        in_specs=[tok(GM_W), tok(CF_W), tok(D_MODEL), st.mod_spec(tm)] + [_resident(a.shape) for a in consts],
        out_specs=tok(D_MODEL),
        out_shape=jax.ShapeDtypeStruct((st.n_tok, D_MODEL), F32),
        compiler_params=_cparams(("parallel",)),
        name="od_out",
    )(ygm, cv, x, mod, *consts)


def _prep_even(w_in, b_gates, conv_k, conv_q, conv_sc):
    wk = w_in[:, 0:ML_KW].astype(BF16)
    wq = w_in[:, EV_Q0:EV_O0].astype(BF16)
    wv = w_in[:, EV_V0:EV_G0].astype(BF16)
    wo = w_in[:, EV_O0:EV_SC0].astype(BF16)
    wsc = jnp.stack([w_in[:, EV_SC0 + i * SC_W:EV_SC0 + (i + 1) * SC_W] for i in range(3)]).astype(BF16)

    def per_head(g):
        lead = g.shape[:-1]
        g = g.reshape(lead + (2, 2, ML_HEADS))
        g = jnp.moveaxis(g, -1, -3).reshape(lead + (ML_HEADS, 4))
        g = jnp.pad(g, [(0, 0)] * (len(lead) + 1) + [(0, LANES - 4)])
        return g.reshape(lead + (ML_HEADS * LANES,))

    wg = per_head(w_in[:, EV_G0:EV_G0 + ML_GATES]).astype(BF16)
    bg = per_head(b_gates.reshape(1, ML_GATES))
    return wk, wq, wv, wo, wsc, wg, bg, conv_k, conv_q, conv_sc


def _zero_state(nb):
    return (jnp.zeros((nb, ML_HEADS, 2, ML_DVX, ML_DK), F32),
            jnp.full((nb, ML_HEADS, 2, 1, LANES), M_INIT, F32))


def kernel(x, c, ctx, c_ctx, w_mod, b_mod, ln1_g, ln1_b, ln2_g, ln2_b, ffn_w_up, ffn_conv, ffn_w_down,
           ev_w_in, ev_b_gates, ev_conv_k, ev_conv_q, ev_norm_g, ev_conv_sc, ev_w_out,
           od_w_in, od_sgu_g, od_sgu_b, od_w_s, od_b_s, od_conv_dw, od_cf_g, od_cf_b, od_w_out):
    nb, seq, d = x.shape
    ctx_len = ctx.shape[1]
    depth = w_mod.shape[0]
    assert d == D_MODEL and seq % TM_LATENT == 0 and ctx_len % ML_CHUNK == 0 and nb + 1 <= 8
    alpha = (2.0 * depth) ** 0.25
    last_even = ((depth - 1) // 2) * 2

    lat = _Stream(nb, seq, TM_LATENT, TM_IN, GRID_W, GRID_W, None)
    cst = _Stream(nb, ctx_len, ctx_len, ctx_len, ctx_len, 1, nb)

    c8 = jnp.zeros((8, d), F32).at[:nb].set(c).at[nb].set(c_ctx)
    mod_all = _modulation(c8, w_mod, b_mod).reshape(depth, 8, 6, d)

    xs = x.reshape(nb * seq, d)
    cs = ctx.reshape(nb * ctx_len, d)
    row = lambda v: v.reshape(1, -1)

    for l in range(depth):
        ctx_out = l < last_even
        j = l // 2
        mod = mod_all[l]
        w_up = ffn_w_up[l].reshape(d, 2, D_FF).transpose(1, 0, 2).astype(BF16)
        c_up = ffn_conv[l].reshape(3, 2, D_FF).transpose(1, 0, 2)
        w_down = ffn_w_down[l].astype(BF16)
        g1, b1, g2, b2 = row(ln1_g[l]), row(ln1_b[l]), row(ln2_g[l]), row(ln2_b[l])

        if l % 2 == 0:
            wts = _prep_even(ev_w_in[j], ev_b_gates[j], ev_conv_k[j], ev_conv_q[j], ev_conv_sc[j])
            w_out = ev_w_out[j].astype(BF16)
            norm_g = row(ev_norm_g[j])

            def even_mixer(st, z, state0, want_y):
                k, qt, vt, so, ysc, g, gt = _ev_in(st, z, mod, wts)
                yml, fin = _mlstm(st, k, qt, vt, so, g, gt, state0, norm_g)
                if not want_y:
                    return None, fin
                return _ev_out(st, yml, ysc, z, mod, w_out, g1, b1, alpha), fin

            c1, fin = even_mixer(cst, cs, _zero_state(nb), ctx_out)
            x1, _ = even_mixer(lat, xs, fin, True)
        else:
            w4 = od_w_in[j].reshape(d, 4, GM_W).transpose(1, 0, 2).astype(BF16)
            w_s = od_w_s[j].astype(BF16)
            b_s_t = od_b_s[j].T
            w_out = od_w_out[j].astype(BF16)

            def odd_mixer(st, z):
                ygm, glu = _od_in(st, z, mod, w4, row(od_sgu_g[j]), row(od_sgu_b[j]), w_s, b_s_t)
                cv = _conv_long(st, glu, od_conv_dw[j])
                return _od_out(st, ygm, cv, z, mod, row(od_cf_g[j]), row(od_cf_b[j]), w_out, g1, b1, alpha)

            x1 = odd_mixer(lat, xs)
            c1 = odd_mixer(cst, cs) if ctx_out else None

        xs = _ffn(lat, x1, mod, w_up, c_up, w_down, g2, b2, alpha)
        if ctx_out:
            cs = _ffn(cst, c1, mod, w_up, c_up, w_down, g2, b2, alpha)

    return xs.reshape(nb, seq, d)
```

```python
import functools

import jax
import jax.numpy as jnp
from jax import lax
from jax.experimental import pallas as pl
from jax.experimental.pallas import tpu as pltpu

F32 = jnp.float32
BF16 = jnp.bfloat16

D_MODEL = 2048
GRID_W = 64
ML_HEADS = 4
ML_DV = 256
ML_DK = 128
ML_VW = ML_HEADS * ML_DV
ML_KW = ML_HEADS * ML_DK
ML_GATES = 16
ML_CHUNK = 128
M_INIT = -1e30
SC_W = 1024
GM_W = 1024
GM_GROUPS = 4
GM_CHUNK = 128
CF_W = 1024
CF_CONV = 31
D_FF = 5632
LN_EPS = 1e-5

EV_V0 = ML_KW
EV_G0 = EV_V0 + ML_VW
EV_Q0 = EV_G0 + ML_GATES
EV_O0 = EV_Q0 + ML_KW
EV_SC0 = EV_O0 + ML_VW

V7X_VMEM_BYTES = 64 * 2**20
VMEM_LIMIT = V7X_VMEM_BYTES - 8 * 2**20
LANES = 128
BF16_ROWS = 16
ML_UNROLL = 8

TM_LATENT = 512
TM_IN = 256
FFN_CHUNK = 512
CONV_ROWS = 64
ML_DVX = ML_DV + BF16_ROWS


def _cparams(sem):
    return pltpu.CompilerParams(dimension_semantics=sem, vmem_limit_bytes=VMEM_LIMIT)


def _silu(z):
    return z * jax.nn.sigmoid(z)


def _gelu(z):
    return 0.5 * z * (1.0 + lax.erf(z * (2.0 ** -0.5)))


def _log_sigmoid(z):
    return jnp.minimum(z, 0.0) - jnp.log1p(jnp.exp(-jnp.abs(z)))


def _ln(z, g, b):
    mu = jnp.mean(z, axis=-1, keepdims=True)
    zc = z - mu
    var = jnp.mean(zc * zc, axis=-1, keepdims=True)
    return zc * lax.rsqrt(var + LN_EPS) * g + b


def _ada(x, mod_ref, slot):
    return x * (1.0 + mod_ref[0, slot + 1:slot + 2, :]) + mod_ref[0, slot:slot + 1, :]


def _conv3(a, w, row_w):
    tm = a.shape[0]
    t = lax.broadcasted_iota(jnp.int32, (tm, 1), 0) % row_w
    prev = jnp.where(t == 0, 0.0, pltpu.roll(a, 1, 0))
    nxt = jnp.where(t == row_w - 1, 0.0, pltpu.roll(a, tm - 1, 0))
    return w[0:1, :] * prev + w[1:2, :] * a + w[2:3, :] * nxt


def _dot(a, b):
    return jnp.dot(a, b, preferred_element_type=F32)


def _tri_sum(tri, z):
    hi = z.astype(BF16)
    r1 = z - hi.astype(F32)
    mid = r1.astype(BF16)
    lo = (r1 - mid.astype(F32)).astype(BF16)
    return _dot(tri, hi) + _dot(tri, mid) + _dot(tri, lo)


def _resident(shape):
    zeros = (0,) * len(shape)
    return pl.BlockSpec(shape, lambda *_: zeros, pipeline_mode=pl.Buffered(1))


def _resident_layer(stacked, layer):
    tail = (0,) * (stacked.ndim - 1)
    return pl.BlockSpec((1,) + stacked.shape[1:], lambda *_: (layer,) + tail, pipeline_mode=pl.Buffered(1))


def _mod_kernel(c_ref, w_ref, b_ref, o_ref):
    s = _silu(c_ref[...]).astype(BF16)
    o_ref[0] = _dot(s, w_ref[0].astype(BF16)) + b_ref[0]


def _modulation(c8, w_mod, b_mod):
    depth, d, n = w_mod.shape
    tn = 1024
    return pl.pallas_call(
        _mod_kernel,
        grid=(depth, n // tn),
        in_specs=[
            pl.BlockSpec((8, d), lambda l, j: (0, 0)),
            pl.BlockSpec((1, d, tn), lambda l, j: (l, 0, j)),
            pl.BlockSpec((1, 1, tn), lambda l, j: (l, 0, j)),
        ],
        out_specs=pl.BlockSpec((1, 8, tn), lambda l, j: (l, 0, j)),
        out_shape=jax.ShapeDtypeStruct((depth, 8, n), F32),
        compiler_params=_cparams(("parallel", "parallel")),
        name="modulation",
    )(c8, w_mod, b_mod.reshape(depth, 1, n))


class _Stream:
    def __init__(self, n_batch, t_batch, tm, tm_in, row_w, long_stride, mod_row):
        self.n_batch = n_batch
        self.t_batch = t_batch
        self.tm = tm
        self.tm_in = tm_in
        self.row_w = row_w
        self.long_stride = long_stride
        self.mod_row = mod_row
        self.n_tok = n_batch * t_batch

    def mod_spec(self, tm):
        if self.mod_row is None:
            tpb = self.t_batch // tm
            return pl.BlockSpec((1, 6, D_MODEL), lambda i, *_: (i // tpb, 0, 0))
        r = self.mod_row
        return pl.BlockSpec((1, 6, D_MODEL), lambda i, *_: (r, 0, 0))


def _ev_in_kernel(x_ref, mod_ref, wk_ref, wq_ref, wv_ref, wo_ref, wsc_ref, wg_ref, bg_ref, ck_ref, cq_ref, csc_ref,
                  k_ref, qt_ref, vt_ref, so_ref, ysc_ref, g_ref, gt_ref, *, row_w):
    L = ML_CHUNK
    tm = x_ref.shape[0]
    chunks = [slice(ci * L, (ci + 1) * L) for ci in range(tm // L)]
    hb = _ada(x_ref[...], mod_ref, 0).astype(BF16)

    k = _silu(_conv3(_dot(hb, wk_ref[...]), ck_ref[...], row_w)) * (ML_DK ** -0.5)
    k_ref[...] = k.astype(BF16)

    q = _silu(_conv3(_dot(hb, wq_ref[...]), cq_ref[...], row_w))
    for ci, rows in enumerate(chunks):
        for f in range(ML_KW // L):
            qt_ref[ci, f * L:(f + 1) * L, :] = q[rows, f * L:(f + 1) * L].T.astype(BF16)

    v = _dot(hb, wv_ref[...])
    for ci, rows in enumerate(chunks):
        for f in range(ML_VW // L):
            vt_ref[ci, f * L:(f + 1) * L, :] = v[rows, f * L:(f + 1) * L].T.astype(BF16)

    so_ref[...] = jax.nn.sigmoid(_dot(hb, wo_ref[...])).astype(BF16)

    gate_c = _dot(hb, wsc_ref[0])
    val = _dot(hb, wsc_ref[2])
    mixed = _conv3(gate_c * val, csc_ref[...], row_w)
    ysc_ref[...] = (_dot(hb, wsc_ref[1]) * mixed).astype(BF16)

    g = _dot(hb, wg_ref[...]) + bg_ref[...]
    lane = lax.broadcasted_iota(jnp.int32, (L, g.shape[1]), 1) % LANES
    r_i = lax.broadcasted_iota(jnp.int32, (L, L), 0)
    c_i = lax.broadcasted_iota(jnp.int32, (L, L), 1)
    lower = (c_i <= r_i).astype(BF16)
    upper = (c_i >= r_i).astype(BF16)
    for ci, rows in enumerate(chunks):
        gc = g[rows, :]
        lf = _log_sigmoid(gc)
        out = jnp.where(lane == 1, _tri_sum(lower, lf), jnp.where(lane == 3, _tri_sum(upper, lf), gc))
        g_ref[rows, :] = out
        for hd in range(ML_HEADS):
            gt_ref[ci, hd] = out[:, hd * LANES:(hd + 1) * LANES].T[0:8, :]


def _ev_in(st, x, mod, wts):
    wk, wq, wv, wo, wsc, wg, bg, ck, cq, csc = wts
    tm = st.tm_in
    nck = tm // ML_CHUNK
    n_chunks = st.n_tok // ML_CHUNK
    tok = lambda w: pl.BlockSpec((tm, w), lambda i: (i, 0))
    return pl.pallas_call(
        functools.partial(_ev_in_kernel, row_w=st.row_w),
        grid=(st.n_tok // tm,),
        in_specs=[tok(D_MODEL), st.mod_spec(tm)] + [_resident(w.shape) for w in wts],
        out_specs=[
            tok(ML_KW),
            pl.BlockSpec((nck, ML_KW, ML_CHUNK), lambda i: (i, 0, 0)),
            pl.BlockSpec((nck, ML_VW, ML_CHUNK), lambda i: (i, 0, 0)),
            tok(ML_VW),
            tok(SC_W),
            tok(ML_HEADS * LANES),
            pl.BlockSpec((nck, ML_HEADS, 8, ML_CHUNK), lambda i: (i, 0, 0, 0)),
        ],
        out_shape=[
            jax.ShapeDtypeStruct((st.n_tok, ML_KW), BF16),
            jax.ShapeDtypeStruct((n_chunks, ML_KW, ML_CHUNK), BF16),
            jax.ShapeDtypeStruct((n_chunks, ML_VW, ML_CHUNK), BF16),
            jax.ShapeDtypeStruct((st.n_tok, ML_VW), BF16),
            jax.ShapeDtypeStruct((st.n_tok, SC_W), BF16),
            jax.ShapeDtypeStruct((st.n_tok, ML_HEADS * LANES), F32),
            jax.ShapeDtypeStruct((n_chunks, ML_HEADS, 8, ML_CHUNK), F32),
        ],
        compiler_params=_cparams(("parallel",)),
        name="ev_in",
    )(x, mod, *wts)


def _mlstm_kernel(k_ref, qt_ref, vt_ref, so_ref, g_ref, gt_ref, c0_ref, m0_ref, ng_ref,
                  y_ref, cf_ref, mf_ref, st_scr, nm_scr, *, nch):
    L = ML_CHUNK
    s_i = lax.broadcasted_iota(jnp.int32, (L, L), 0)
    t_i = lax.broadcasted_iota(jnp.int32, (L, L), 1)
    row_8 = lax.broadcasted_iota(jnp.int32, (8, LANES), 0)

    def gate_rows(g8, d):
        ig_row, b_row = g8[2 * d:2 * d + 1, :], g8[2 * d + 1:2 * d + 2, :]
        b_last = b_row[:, L - 1:L] if d == 0 else b_row[:, 0:1]
        return ig_row, b_row, b_last

    def local_pass(c, carry):
        r = pl.multiple_of(c * L, L)
        g8 = gt_ref[c, 0]
        kc = k_ref[pl.ds(r, L), :]
        vtc = vt_ref[c].astype(F32)
        for d in range(2):
            ig_row, b_row, b_last = gate_rows(g8, d)
            a_row = b_last - b_row + ig_row
            m_loc = jnp.max(a_row, axis=-1, keepdims=True)
            w_row = jnp.exp(a_row - m_loc)
            lhs = jnp.concatenate([vtc * w_row, jnp.broadcast_to(w_row, (BF16_ROWS, L))], axis=0)
            st_scr[d, c] = _dot(lhs.astype(BF16), kc)
            nm_scr[d, c] = jnp.where(row_8 == 0, m_loc, b_last)
        return carry

    lax.fori_loop(0, nch, local_pass, 0, unroll=min(ML_UNROLL, nch))

    for d in range(2):
        def scan_pass(it, carry, d=d):
            cx, m_prev = carry
            c = it if d == 0 else nch - 1 - it
            loc = st_scr[d, c]
            blk = nm_scr[d, c]
            m_loc, b_last = blk[0:1, 0:1], blk[1:2, 0:1]
            st_scr[d, c] = cx
            nm_scr[d, c] = jnp.broadcast_to(m_prev, (8, LANES))
            m_new = jnp.maximum(b_last + m_prev, m_loc)
            return jnp.exp(b_last + m_prev - m_new) * cx + jnp.exp(m_loc - m_new) * loc, m_new

        cx, m_fin = lax.fori_loop(0, nch, scan_pass, (c0_ref[0, 0, d], m0_ref[0, 0, d][:, 0:1]))
        cf_ref[0, 0, d] = cx
        mf_ref[0, 0, d] = jnp.broadcast_to(m_fin, (1, LANES))

    def output_pass(c, carry):
        r = pl.multiple_of(c * L, L)
        g8 = gt_ref[c, 0]
        gch = g_ref[pl.ds(r, L), :]
        qtc = qt_ref[c]
        vtc = vt_ref[c]
        sraw = _dot(k_ref[pl.ds(r, L), :], qtc)
        h = None
        for d in range(2):
            ig_row, b_row, _ = gate_rows(g8, d)
            src_col = gch[:, 2 * d:2 * d + 1] - gch[:, 2 * d + 1:2 * d + 2]
            visible = (s_i <= t_i) if d == 0 else (s_i >= t_i)
            log_intra = jnp.where(visible, src_col + b_row, -jnp.inf)
            m_prev = nm_scr[d, c][0:1, 0:1]
            log_inter = b_row + m_prev
            m_t = jnp.maximum(log_inter, jnp.max(log_intra, axis=0, keepdims=True))
            s = sraw * jnp.exp(log_intra - m_t)
            w_inter = jnp.exp(log_inter - m_t)
            inter = _dot(st_scr[d, c].astype(BF16), qtc)
            num = _dot(vtc, s.astype(BF16)) + w_inter * inter[0:ML_DV, :]
            den = jnp.sum(s, axis=0, keepdims=True) + w_inter * inter[ML_DV:ML_DV + 1, :]
            hd = num * (1.0 / jnp.maximum(jnp.abs(den), jnp.exp(-m_t)))
            h = hd if h is None else h + hd
        mu = jnp.mean(h, axis=0, keepdims=True)
        hc = h - mu
        var = jnp.mean(hc * hc, axis=0, keepdims=True)
        hn = (hc * lax.rsqrt(var + LN_EPS)).T
        y_ref[pl.ds(r, L), :] = (hn * ng_ref[...] * so_ref[pl.ds(r, L), :].astype(F32)).astype(BF16)
        return carry

    lax.fori_loop(0, nch, output_pass, 0, unroll=min(ML_UNROLL, nch))


def _mlstm(st, k, qt, vt, so, g, gt, state0, norm_g):
    tb = st.t_batch
    nch = tb // ML_CHUNK
    nb, nh = st.n_batch, ML_HEADS
    c0, m0 = state0
    state_specs = [
        pl.BlockSpec((1, 1, 2, ML_DVX, ML_DK), lambda b, h: (b, h, 0, 0, 0)),
        pl.BlockSpec((1, 1, 2, 1, LANES), lambda b, h: (b, h, 0, 0, 0)),
    ]
    y, cf, mf = pl.pallas_call(
        functools.partial(_mlstm_kernel, nch=nch),
        grid=(nb, nh),
        in_specs=[
            pl.BlockSpec((tb, ML_DK), lambda b, h: (b, h)),
            pl.BlockSpec((nch, ML_DK, ML_CHUNK), lambda b, h: (b, h, 0)),
            pl.BlockSpec((nch, ML_DV, ML_CHUNK), lambda b, h: (b, h, 0)),
            pl.BlockSpec((tb, ML_DV), lambda b, h: (b, h)),
            pl.BlockSpec((tb, LANES), lambda b, h: (b, h)),
            pl.BlockSpec((nch, 1, 8, ML_CHUNK), lambda b, h: (b, h, 0, 0)),
        ] + state_specs + [pl.BlockSpec((1, ML_DV), lambda b, h: (0, h))],
        out_specs=[pl.BlockSpec((tb, ML_DV), lambda b, h: (b, h))] + state_specs,
        out_shape=[
            jax.ShapeDtypeStruct((st.n_tok, ML_VW), BF16),
            jax.ShapeDtypeStruct((nb, nh, 2, ML_DVX, ML_DK), F32),
            jax.ShapeDtypeStruct((nb, nh, 2, 1, LANES), F32),
        ],
        scratch_shapes=[
            pltpu.VMEM((2, nch, ML_DVX, ML_DK), F32),
            pltpu.VMEM((2, nch, 8, LANES), F32),
        ],
        compiler_params=_cparams(("parallel", "parallel")),
        name="mlstm",
    )(k, qt, vt, so, g, gt, c0, m0, norm_g)
    return y, (cf, mf)


def _residual_ln(x, gate, y, g_ref, b_ref, alpha):
    return _ln(alpha * x + gate * y, g_ref[...], b_ref[...])


def _ev_out_kernel(yml_ref, ysc_ref, x_ref, mod_ref, w_ref, g_ref, b_ref, o_ref, *, alpha):
    y = _dot(yml_ref[...], w_ref[0, 0:ML_VW, :]) + _dot(ysc_ref[...], w_ref[0, ML_VW:, :])
    o_ref[...] = _residual_ln(x_ref[...], mod_ref[0, 2:3, :], y, g_ref, b_ref, alpha)


def _ev_out(st, yml, ysc, x, mod, layer, w_out, ln_g, ln_b, alpha):
    tm = st.tm
    tok = lambda w: pl.BlockSpec((tm, w), lambda i: (i, 0))
    return pl.pallas_call(
        functools.partial(_ev_out_kernel, alpha=alpha),
        grid=(st.n_tok // tm,),
        in_specs=[tok(ML_VW), tok(SC_W), tok(D_MODEL), st.mod_spec(tm), _resident_layer(w_out, layer),
                  _resident(ln_g.shape), _resident(ln_b.shape)],
        out_specs=tok(D_MODEL),
        out_shape=jax.ShapeDtypeStruct((st.n_tok, D_MODEL), F32),
        compiler_params=_cparams(("parallel",)),
        name="ev_out",
    )(yml, ysc, x, mod, w_out, ln_g, ln_b)


def _ffn_kernel(x_ref, mod_ref, wa_ref, wg_ref, ca_ref, cg_ref, wdn_ref, g_ref, b_ref, o_ref,
                h_scr, acc, *, row_w, alpha):
    j = pl.program_id(1)

    @pl.when(j == 0)
    def _():
        h_scr[...] = _ada(x_ref[...], mod_ref, 3).astype(BF16)
        acc[...] = jnp.zeros_like(acc)

    h = h_scr[...]
    a = _conv3(_dot(h, wa_ref[0]), ca_ref[0], row_w)
    g = _conv3(_dot(h, wg_ref[0]), cg_ref[0], row_w)
    acc[...] += _dot((_silu(g) * a).astype(BF16), wdn_ref[0])

    @pl.when(j == pl.num_programs(1) - 1)
    def _():
        o_ref[...] = _residual_ln(x_ref[...], mod_ref[0, 5:6, :], acc[...], g_ref, b_ref, alpha)


def _ffn(st, x, mod, layer, w_up, c_up, w_down, ln_g, ln_b, alpha):
    tm, fc = st.tm, FFN_CHUNK
    nf = D_FF // fc
    return pl.pallas_call(
        functools.partial(_ffn_kernel, row_w=st.row_w, alpha=alpha),
        grid=(st.n_tok // tm, nf),
        in_specs=[
            pl.BlockSpec((tm, D_MODEL), lambda i, j: (i, 0)),
            st.mod_spec(tm),
            pl.BlockSpec((1, D_MODEL, fc), lambda i, j: (layer, 0, j)),
            pl.BlockSpec((1, D_MODEL, fc), lambda i, j: (layer, 0, nf + j)),
            pl.BlockSpec((1, 3, fc), lambda i, j: (layer, 0, j)),
            pl.BlockSpec((1, 3, fc), lambda i, j: (layer, 0, nf + j)),
            pl.BlockSpec((1, fc, D_MODEL), lambda i, j: (layer, j, 0)),
            pl.BlockSpec((1, D_MODEL), lambda i, j: (0, 0)),
            pl.BlockSpec((1, D_MODEL), lambda i, j: (0, 0)),
        ],
        out_specs=pl.BlockSpec((tm, D_MODEL), lambda i, j: (i, 0)),
        out_shape=jax.ShapeDtypeStruct((st.n_tok, D_MODEL), F32),
        scratch_shapes=[pltpu.VMEM((tm, D_MODEL), BF16), pltpu.VMEM((tm, D_MODEL), F32)],
        compiler_params=_cparams(("parallel", "arbitrary")),
        name="ffn",
    )(x, mod, w_up, w_up, c_up, c_up, w_down, ln_g, ln_b)


def _od_in_kernel(x_ref, mod_ref, w_ref, sg_ref, sb_ref, ws_ref, bs_ref, ygm_ref, glu_ref):
    gw = GM_W // GM_GROUPS
    hb = _ada(x_ref[...], mod_ref, 0).astype(BF16)
    part = lambda p: w_ref[0, :, p * GM_W:(p + 1) * GM_W]
    glu_ref[...] = (_dot(hb, part(2)) * jax.nn.sigmoid(_dot(hb, part(3)))).astype(BF16)
    vn = _ln(_gelu(_dot(hb, part(1))), sg_ref[...], sb_ref[...]).astype(BF16)
    u = _gelu(_dot(hb, part(0)))
    for ci in range(x_ref.shape[0] // GM_CHUNK):
        rows = slice(ci * GM_CHUNK, (ci + 1) * GM_CHUNK)
        for g in range(GM_GROUPS):
            cols = slice(g * gw, (g + 1) * gw)
            mixed = _dot(ws_ref[g], vn[rows, cols]) + bs_ref[:, g:g + 1]
            ygm_ref[rows, cols] = (u[rows, cols] * mixed).astype(BF16)


def _od_in(st, x, mod, layer, w_in, sgu_g, sgu_b, w_s, b_s_t):
    tm = st.tm_in
    tok = lambda w: pl.BlockSpec((tm, w), lambda i: (i, 0))
    out = jax.ShapeDtypeStruct((st.n_tok, GM_W), BF16)
    consts = (sgu_g, sgu_b, w_s, b_s_t)
    return pl.pallas_call(
        _od_in_kernel,
        grid=(st.n_tok // tm,),
        in_specs=[tok(D_MODEL), st.mod_spec(tm), _resident_layer(w_in, layer)] + [_resident(a.shape) for a in consts],
        out_specs=[tok(GM_W), tok(CF_W)],
        out_shape=[out, out],
        compiler_params=_cparams(("parallel",)),
        name="od_in",
    )(x, mod, w_in, *consts)


def _conv_long_kernel(x_ref, w_ref, o_ref, pad, *, stride, tb):
    half = (CF_CONV // 2) * stride
    cb = x_ref.shape[1]
    pad[0:half, :] = jnp.zeros((half, cb), F32)
    pad[half + tb:, :] = jnp.zeros((half, cb), F32)
    pad[half:half + tb, :] = x_ref[...].astype(F32)
    w = w_ref[...]

    def block(r0, aligned):
        acc = jnp.zeros((CONV_ROWS, cb), F32)
        for j in range(CF_CONV):
            start = r0 + j * stride
            acc = acc + w[j:j + 1, :] * pad[pl.ds(pl.multiple_of(start, 8) if aligned else start, CONV_ROWS), :]
        o_ref[pl.ds(r0, CONV_ROWS), :] = acc.astype(o_ref.dtype)

    n_blocks = tb // CONV_ROWS
    if stride % 8 == 0:
        def body(i, carry):
            block(pl.multiple_of(i * CONV_ROWS, CONV_ROWS), True)
            return carry
        lax.fori_loop(0, n_blocks, body, 0)
    else:
        for i in range(n_blocks):
            block(i * CONV_ROWS, False)


def _conv_long(st, x, w):
    tb, cb = st.t_batch, 256
    stride = st.long_stride
    return pl.pallas_call(
        functools.partial(_conv_long_kernel, stride=stride, tb=tb),
        grid=(st.n_batch, CF_W // cb),
        in_specs=[
            pl.BlockSpec((tb, cb), lambda b, j: (b, j)),
            pl.BlockSpec((CF_CONV, cb), lambda b, j: (0, j)),
        ],
        out_specs=pl.BlockSpec((tb, cb), lambda b, j: (b, j)),
        out_shape=jax.ShapeDtypeStruct((st.n_tok, CF_W), BF16),
        scratch_shapes=[pltpu.VMEM((tb + 2 * (CF_CONV // 2) * stride, cb), F32)],
        compiler_params=_cparams(("parallel", "parallel")),
        name="conv_long",
    )(x, w)


def _od_out_kernel(ygm_ref, cv_ref, x_ref, mod_ref, cg_ref, cb_ref, w_ref, g_ref, b_ref, o_ref, *, alpha):
    ycf = _silu(_ln(cv_ref[...].astype(F32), cg_ref[...], cb_ref[...])).astype(BF16)
    y = _dot(ygm_ref[...], w_ref[0, 0:GM_W, :]) + _dot(ycf, w_ref[0, GM_W:, :])
    o_ref[...] = _residual_ln(x_ref[...], mod_ref[0, 2:3, :], y, g_ref, b_ref, alpha)


def _od_out(st, ygm, cv, x, mod, layer, cf_g, cf_b, w_out, ln_g, ln_b, alpha):
    tm = st.tm
    tok = lambda w: pl.BlockSpec((tm, w), lambda i: (i, 0))
    return pl.pallas_call(
        functools.partial(_od_out_kernel, alpha=alpha),
        grid=(st.n_tok // tm,),
        in_specs=[tok(GM_W), tok(CF_W), tok(D_MODEL), st.mod_spec(tm), _resident(cf_g.shape), _resident(cf_b.shape),
                  _resident_layer(w_out, layer), _resident(ln_g.shape), _resident(ln_b.shape)],
        out_specs=tok(D_MODEL),
        out_shape=jax.ShapeDtypeStruct((st.n_tok, D_MODEL), F32),
        compiler_params=_cparams(("parallel",)),
        name="od_out",
    )(ygm, cv, x, mod, cf_g, cf_b, w_out, ln_g, ln_b)


def _prep_even(w_in, b_gates, conv_k, conv_q, conv_sc):
    wk = w_in[:, 0:ML_KW]
    wq = w_in[:, EV_Q0:EV_O0]
    wv = w_in[:, EV_V0:EV_G0]
    wo = w_in[:, EV_O0:EV_SC0]
    wsc = jnp.stack([w_in[:, EV_SC0 + i * SC_W:EV_SC0 + (i + 1) * SC_W] for i in range(3)])

    def per_head(g):
        lead = g.shape[:-1]
        g = g.reshape(lead + (2, 2, ML_HEADS))
        g = jnp.moveaxis(g, -1, -3).reshape(lead + (ML_HEADS, 4))
        g = jnp.pad(g, [(0, 0)] * (len(lead) + 1) + [(0, LANES - 4)])
        return g.reshape(lead + (ML_HEADS * LANES,))

    wg = per_head(w_in[:, EV_G0:EV_G0 + ML_GATES]).astype(BF16)
    bg = per_head(b_gates.reshape(1, ML_GATES))
    return wk, wq, wv, wo, wsc, wg, bg, conv_k, conv_q, conv_sc


def _zero_state(nb):
    return (jnp.zeros((nb, ML_HEADS, 2, ML_DVX, ML_DK), F32),
            jnp.full((nb, ML_HEADS, 2, 1, LANES), M_INIT, F32))


def kernel(x, c, ctx, c_ctx, w_mod, b_mod, ln1_g, ln1_b, ln2_g, ln2_b, ffn_w_up, ffn_conv, ffn_w_down,
           ev_w_in, ev_b_gates, ev_conv_k, ev_conv_q, ev_norm_g, ev_conv_sc, ev_w_out,
           od_w_in, od_sgu_g, od_sgu_b, od_w_s, od_b_s, od_conv_dw, od_cf_g, od_cf_b, od_w_out):
    nb, seq, d = x.shape
    ctx_len = ctx.shape[1]
    depth = w_mod.shape[0]
    assert d == D_MODEL and seq % TM_LATENT == 0 and ctx_len % ML_CHUNK == 0 and nb + 1 <= 8
    alpha = (2.0 * depth) ** 0.25
    last_even = ((depth - 1) // 2) * 2

    lat = _Stream(nb, seq, TM_LATENT, TM_IN, GRID_W, GRID_W, None)
    cst = _Stream(nb, ctx_len, ctx_len, ctx_len, ctx_len, 1, nb)

    c8 = jnp.zeros((8, d), F32).at[:nb].set(c).at[nb].set(c_ctx)
    mod_all = _modulation(c8, w_mod, b_mod).reshape(depth, 8, 6, d)

    xs = x.reshape(nb * seq, d)
    cs = ctx.reshape(nb * ctx_len, d)
    row = lambda v: v.reshape(1, -1)

    ffn_up_b, ffn_down_b = ffn_w_up.astype(BF16), ffn_w_down.astype(BF16)
    ev_in_b, ev_out_b = ev_w_in.astype(BF16), ev_w_out.astype(BF16)
    od_in_b, od_out_b, od_ws_b = od_w_in.astype(BF16), od_w_out.astype(BF16), od_w_s.astype(BF16)

    for l in range(depth):
        ctx_out = l < last_even
        j = l // 2
        mod = mod_all[l]
        g1, b1, g2, b2 = row(ln1_g[l]), row(ln1_b[l]), row(ln2_g[l]), row(ln2_b[l])

        if l % 2 == 0:
            wts = _prep_even(ev_in_b[j], ev_b_gates[j], ev_conv_k[j], ev_conv_q[j], ev_conv_sc[j])
            norm_g = row(ev_norm_g[j])

            def even_mixer(st, z, state0, want_y):
                k, qt, vt, so, ysc, g, gt = _ev_in(st, z, mod, wts)
                yml, fin = _mlstm(st, k, qt, vt, so, g, gt, state0, norm_g)
                if not want_y:
                    return None, fin
                return _ev_out(st, yml, ysc, z, mod, j, ev_out_b, g1, b1, alpha), fin

            c1, fin = even_mixer(cst, cs, _zero_state(nb), ctx_out)
            x1, _ = even_mixer(lat, xs, fin, True)
        else:
            b_s_t = od_b_s[j].T

            def odd_mixer(st, z):
                ygm, glu = _od_in(st, z, mod, j, od_in_b, row(od_sgu_g[j]), row(od_sgu_b[j]), od_ws_b[j], b_s_t)
                cv = _conv_long(st, glu, od_conv_dw[j])
                return _od_out(st, ygm, cv, z, mod, j, row(od_cf_g[j]), row(od_cf_b[j]), od_out_b, g1, b1, alpha)

            x1 = odd_mixer(lat, xs)
            c1 = odd_mixer(cst, cs) if ctx_out else None

        xs = _ffn(lat, x1, mod, l, ffn_up_b, ffn_conv, ffn_down_b, g2, b2, alpha)
        if ctx_out:
            cs = _ffn(cst, c1, mod, l, ffn_up_b, ffn_conv, ffn_down_b, g2, b2, alpha)

    return xs.reshape(nb, seq, d)
```

```python
import functools

import jax
import jax.numpy as jnp
from jax import lax
from jax.experimental import pallas as pl
from jax.experimental.pallas import tpu as pltpu

F32 = jnp.float32
BF16 = jnp.bfloat16

D_MODEL = 2048
GRID_W = 64
ML_HEADS = 4
ML_DV = 256
ML_DK = 128
ML_VW = ML_HEADS * ML_DV
ML_KW = ML_HEADS * ML_DK
ML_GATES = 16
ML_CHUNK = 128
M_INIT = -1e30
SC_W = 1024
GM_W = 1024
GM_GROUPS = 4
GM_CHUNK = 128
CF_W = 1024
CF_CONV = 31
D_FF = 5632
LN_EPS = 1e-5

EV_V0 = ML_KW
EV_G0 = EV_V0 + ML_VW
EV_Q0 = EV_G0 + ML_GATES
EV_O0 = EV_Q0 + ML_KW
EV_SC0 = EV_O0 + ML_VW

V7X_VMEM_BYTES = 64 * 2**20
VMEM_LIMIT = V7X_VMEM_BYTES - 8 * 2**20
LANES = 128
BF16_ROWS = 16
ML_UNROLL = 8

TM_LATENT = 512
TM_IN = 256
FFN_CHUNK = 512
OUT_ROWS = 128
FFN_ROWS = 128
CONV_ROWS = 64
ML_DVX = ML_DV + BF16_ROWS


def _cparams(sem):
    return pltpu.CompilerParams(dimension_semantics=sem, vmem_limit_bytes=VMEM_LIMIT)


def _silu(z):
    return z * jax.nn.sigmoid(z)


def _gelu(z):
    return 0.5 * z * (1.0 + lax.erf(z * (2.0 ** -0.5)))


def _log_sigmoid(z):
    return jnp.minimum(z, 0.0) - jnp.log1p(jnp.exp(-jnp.abs(z)))


def _ln(z, g, b):
    mu = jnp.mean(z, axis=-1, keepdims=True)
    zc = z - mu
    var = jnp.mean(zc * zc, axis=-1, keepdims=True)
    return zc * lax.rsqrt(var + LN_EPS) * g + b


def _ada(x, mod_ref, slot):
    return x * (1.0 + mod_ref[0, slot + 1:slot + 2, :]) + mod_ref[0, slot:slot + 1, :]


def _conv3(a, w, row_w):
    tm = a.shape[0]
    t = lax.broadcasted_iota(jnp.int32, (tm, 1), 0) % row_w
    prev = jnp.where(t == 0, 0.0, pltpu.roll(a, 1, 0))
    nxt = jnp.where(t == row_w - 1, 0.0, pltpu.roll(a, tm - 1, 0))
    return w[0:1, :] * prev + w[1:2, :] * a + w[2:3, :] * nxt


def _dot(a, b):
    return jnp.dot(a, b, preferred_element_type=F32)


def _tri_sum(tri, z):
    hi = z.astype(BF16)
    r1 = z - hi.astype(F32)
    mid = r1.astype(BF16)
    lo = (r1 - mid.astype(F32)).astype(BF16)
    return _dot(tri, hi) + _dot(tri, mid) + _dot(tri, lo)


def _resident(shape):
    zeros = (0,) * len(shape)
    return pl.BlockSpec(shape, lambda *_: zeros, pipeline_mode=pl.Buffered(1))


def _resident_layer(stacked, layer):
    tail = (0,) * (stacked.ndim - 1)
    return pl.BlockSpec((1,) + stacked.shape[1:], lambda *_: (layer,) + tail, pipeline_mode=pl.Buffered(1))


def _mod_kernel(c_ref, w_ref, b_ref, o_ref):
    s = _silu(c_ref[...]).astype(BF16)
    o_ref[0] = _dot(s, w_ref[0].astype(BF16)) + b_ref[0]


def _modulation(c8, w_mod, b_mod):
    depth, d, n = w_mod.shape
    tn = 1024
    return pl.pallas_call(
        _mod_kernel,
        grid=(depth, n // tn),
        in_specs=[
            pl.BlockSpec((8, d), lambda l, j: (0, 0)),
            pl.BlockSpec((1, d, tn), lambda l, j: (l, 0, j)),
            pl.BlockSpec((1, 1, tn), lambda l, j: (l, 0, j)),
        ],
        out_specs=pl.BlockSpec((1, 8, tn), lambda l, j: (l, 0, j)),
        out_shape=jax.ShapeDtypeStruct((depth, 8, n), F32),
        compiler_params=_cparams(("parallel", "parallel")),
        name="modulation",
    )(c8, w_mod, b_mod.reshape(depth, 1, n))


class _Stream:
    def __init__(self, n_batch, t_batch, tm, tm_in, row_w, long_stride, mod_row):
        self.n_batch = n_batch
        self.t_batch = t_batch
        self.tm = tm
        self.tm_in = tm_in
        self.row_w = row_w
        self.long_stride = long_stride
        self.mod_row = mod_row
        self.n_tok = n_batch * t_batch

    def mod_spec(self, tm):
        if self.mod_row is None:
            tpb = self.t_batch // tm
            return pl.BlockSpec((1, 6, D_MODEL), lambda i, *_: (i // tpb, 0, 0))
        r = self.mod_row
        return pl.BlockSpec((1, 6, D_MODEL), lambda i, *_: (r, 0, 0))


def _ev_in_kernel(x_ref, mod_ref, wk_ref, wq_ref, wv_ref, wo_ref, wsc_ref, wg_ref, bg_ref, ck_ref, cq_ref, csc_ref,
                  k_ref, qt_ref, vt_ref, so_ref, ysc_ref, g_ref, gt_ref, *, row_w):
    L = ML_CHUNK
    tm = x_ref.shape[0]
    chunks = [slice(ci * L, (ci + 1) * L) for ci in range(tm // L)]
    hb = _ada(x_ref[...], mod_ref, 0).astype(BF16)

    k = _silu(_conv3(_dot(hb, wk_ref[...]), ck_ref[...], row_w)) * (ML_DK ** -0.5)
    k_ref[...] = k.astype(BF16)

    q = _silu(_conv3(_dot(hb, wq_ref[...]), cq_ref[...], row_w))
    for ci, rows in enumerate(chunks):
        for f in range(ML_KW // L):
            qt_ref[ci, f * L:(f + 1) * L, :] = q[rows, f * L:(f + 1) * L].T.astype(BF16)

    v = _dot(hb, wv_ref[...])
    for ci, rows in enumerate(chunks):
        for f in range(ML_VW // L):
            vt_ref[ci, f * L:(f + 1) * L, :] = v[rows, f * L:(f + 1) * L].T.astype(BF16)

    so_ref[...] = jax.nn.sigmoid(_dot(hb, wo_ref[...])).astype(BF16)

    gate_c = _dot(hb, wsc_ref[0])
    val = _dot(hb, wsc_ref[2])
    mixed = _conv3(gate_c * val, csc_ref[...], row_w)
    ysc_ref[...] = (_dot(hb, wsc_ref[1]) * mixed).astype(BF16)

    g = _dot(hb, wg_ref[...]) + bg_ref[...]
    lane = lax.broadcasted_iota(jnp.int32, (L, g.shape[1]), 1) % LANES
    r_i = lax.broadcasted_iota(jnp.int32, (L, L), 0)
    c_i = lax.broadcasted_iota(jnp.int32, (L, L), 1)
    lower = (c_i <= r_i).astype(BF16)
    upper = (c_i >= r_i).astype(BF16)
    for ci, rows in enumerate(chunks):
        gc = g[rows, :]
        lf = _log_sigmoid(gc)
        out = jnp.where(lane == 1, _tri_sum(lower, lf), jnp.where(lane == 3, _tri_sum(upper, lf), gc))
        g_ref[rows, :] = out
        for hd in range(ML_HEADS):
            gt_ref[ci, hd] = out[:, hd * LANES:(hd + 1) * LANES].T[0:8, :]


def _ev_in(st, x, mod, wts):
    wk, wq, wv, wo, wsc, wg, bg, ck, cq, csc = wts
    tm = st.tm_in
    nck = tm // ML_CHUNK
    n_chunks = st.n_tok // ML_CHUNK
    tok = lambda w: pl.BlockSpec((tm, w), lambda i: (i, 0))
    return pl.pallas_call(
        functools.partial(_ev_in_kernel, row_w=st.row_w),
        grid=(st.n_tok // tm,),
        in_specs=[tok(D_MODEL), st.mod_spec(tm)] + [_resident(w.shape) for w in wts],
        out_specs=[
            tok(ML_KW),
            pl.BlockSpec((nck, ML_KW, ML_CHUNK), lambda i: (i, 0, 0)),
            pl.BlockSpec((nck, ML_VW, ML_CHUNK), lambda i: (i, 0, 0)),
            tok(ML_VW),
            tok(SC_W),
            tok(ML_HEADS * LANES),
            pl.BlockSpec((nck, ML_HEADS, 8, ML_CHUNK), lambda i: (i, 0, 0, 0)),
        ],
        out_shape=[
            jax.ShapeDtypeStruct((st.n_tok, ML_KW), BF16),
            jax.ShapeDtypeStruct((n_chunks, ML_KW, ML_CHUNK), BF16),
            jax.ShapeDtypeStruct((n_chunks, ML_VW, ML_CHUNK), BF16),
            jax.ShapeDtypeStruct((st.n_tok, ML_VW), BF16),
            jax.ShapeDtypeStruct((st.n_tok, SC_W), BF16),
            jax.ShapeDtypeStruct((st.n_tok, ML_HEADS * LANES), F32),
            jax.ShapeDtypeStruct((n_chunks, ML_HEADS, 8, ML_CHUNK), F32),
        ],
        compiler_params=_cparams(("parallel",)),
        name="ev_in",
    )(x, mod, *wts)


def _mlstm_kernel(k_ref, qt_ref, vt_ref, so_ref, g_ref, gt_ref, c0_ref, m0_ref, ng_ref,
                  y_ref, cf_ref, mf_ref, st_scr, nm_scr, *, nch):
    L = ML_CHUNK
    s_i = lax.broadcasted_iota(jnp.int32, (L, L), 0)
    t_i = lax.broadcasted_iota(jnp.int32, (L, L), 1)
    row_8 = lax.broadcasted_iota(jnp.int32, (8, LANES), 0)

    def gate_rows(g8, d):
        ig_row, b_row = g8[2 * d:2 * d + 1, :], g8[2 * d + 1:2 * d + 2, :]
        b_last = b_row[:, L - 1:L] if d == 0 else b_row[:, 0:1]
        return ig_row, b_row, b_last

    def local_pass(c, carry):
        r = pl.multiple_of(c * L, L)
        g8 = gt_ref[c, 0]
        kc = k_ref[pl.ds(r, L), :]
        vtc = vt_ref[c].astype(F32)
        for d in range(2):
            ig_row, b_row, b_last = gate_rows(g8, d)
            a_row = b_last - b_row + ig_row
            m_loc = jnp.max(a_row, axis=-1, keepdims=True)
            w_row = jnp.exp(a_row - m_loc)
            lhs = jnp.concatenate([vtc * w_row, jnp.broadcast_to(w_row, (BF16_ROWS, L))], axis=0)
            st_scr[d, c] = _dot(lhs.astype(BF16), kc)
            nm_scr[d, c] = jnp.where(row_8 == 0, m_loc, b_last)
        return carry

    lax.fori_loop(0, nch, local_pass, 0, unroll=min(ML_UNROLL, nch))

    for d in range(2):
        def scan_pass(it, carry, d=d):
            cx, m_prev = carry
            c = it if d == 0 else nch - 1 - it
            loc = st_scr[d, c]
            blk = nm_scr[d, c]
            m_loc, b_last = blk[0:1, 0:1], blk[1:2, 0:1]
            st_scr[d, c] = cx
            nm_scr[d, c] = jnp.broadcast_to(m_prev, (8, LANES))
            m_new = jnp.maximum(b_last + m_prev, m_loc)
            return jnp.exp(b_last + m_prev - m_new) * cx + jnp.exp(m_loc - m_new) * loc, m_new

        cx, m_fin = lax.fori_loop(0, nch, scan_pass, (c0_ref[0, 0, d], m0_ref[0, 0, d][:, 0:1]))
        cf_ref[0, 0, d] = cx
        mf_ref[0, 0, d] = jnp.broadcast_to(m_fin, (1, LANES))

    def output_pass(c, carry):
        r = pl.multiple_of(c * L, L)
        g8 = gt_ref[c, 0]
        gch = g_ref[pl.ds(r, L), :]
        qtc = qt_ref[c]
        vtc = vt_ref[c]
        sraw = _dot(k_ref[pl.ds(r, L), :], qtc)
        h = None
        for d in range(2):
            ig_row, b_row, _ = gate_rows(g8, d)
            src_col = gch[:, 2 * d:2 * d + 1] - gch[:, 2 * d + 1:2 * d + 2]
            visible = (s_i <= t_i) if d == 0 else (s_i >= t_i)
            log_intra = jnp.where(visible, src_col + b_row, -jnp.inf)
            m_prev = nm_scr[d, c][0:1, 0:1]
            log_inter = b_row + m_prev
            m_t = jnp.maximum(log_inter, jnp.max(log_intra, axis=0, keepdims=True))
            s = sraw * jnp.exp(log_intra - m_t)
            w_inter = jnp.exp(log_inter - m_t)
            inter = _dot(st_scr[d, c].astype(BF16), qtc)
            num = _dot(vtc, s.astype(BF16)) + w_inter * inter[0:ML_DV, :]
            den = jnp.sum(s, axis=0, keepdims=True) + w_inter * inter[ML_DV:ML_DV + 1, :]
            hd = num * (1.0 / jnp.maximum(jnp.abs(den), jnp.exp(-m_t)))
            h = hd if h is None else h + hd
        mu = jnp.mean(h, axis=0, keepdims=True)
        hc = h - mu
        var = jnp.mean(hc * hc, axis=0, keepdims=True)
        hn = (hc * lax.rsqrt(var + LN_EPS)).T
        y_ref[pl.ds(r, L), :] = (hn * ng_ref[...] * so_ref[pl.ds(r, L), :].astype(F32)).astype(BF16)
        return carry

    lax.fori_loop(0, nch, output_pass, 0, unroll=min(ML_UNROLL, nch))


def _mlstm(st, k, qt, vt, so, g, gt, state0, norm_g):
    tb = st.t_batch
    nch = tb // ML_CHUNK
    nb, nh = st.n_batch, ML_HEADS
    c0, m0 = state0
    state_specs = [
        pl.BlockSpec((1, 1, 2, ML_DVX, ML_DK), lambda b, h: (b, h, 0, 0, 0)),
        pl.BlockSpec((1, 1, 2, 1, LANES), lambda b, h: (b, h, 0, 0, 0)),
    ]
    y, cf, mf = pl.pallas_call(
        functools.partial(_mlstm_kernel, nch=nch),
        grid=(nb, nh),
        in_specs=[
            pl.BlockSpec((tb, ML_DK), lambda b, h: (b, h)),
            pl.BlockSpec((nch, ML_DK, ML_CHUNK), lambda b, h: (b, h, 0)),
            pl.BlockSpec((nch, ML_DV, ML_CHUNK), lambda b, h: (b, h, 0)),
            pl.BlockSpec((tb, ML_DV), lambda b, h: (b, h)),
            pl.BlockSpec((tb, LANES), lambda b, h: (b, h)),
            pl.BlockSpec((nch, 1, 8, ML_CHUNK), lambda b, h: (b, h, 0, 0)),
        ] + state_specs + [pl.BlockSpec((1, ML_DV), lambda b, h: (0, h))],
        out_specs=[pl.BlockSpec((tb, ML_DV), lambda b, h: (b, h))] + state_specs,
        out_shape=[
            jax.ShapeDtypeStruct((st.n_tok, ML_VW), BF16),
            jax.ShapeDtypeStruct((nb, nh, 2, ML_DVX, ML_DK), F32),
            jax.ShapeDtypeStruct((nb, nh, 2, 1, LANES), F32),
        ],
        scratch_shapes=[
            pltpu.VMEM((2, nch, ML_DVX, ML_DK), F32),
            pltpu.VMEM((2, nch, 8, LANES), F32),
        ],
        compiler_params=_cparams(("parallel", "parallel")),
        name="mlstm",
    )(k, qt, vt, so, g, gt, c0, m0, norm_g)
    return y, (cf, mf)


def _residual_ln(x, gate, y, g_ref, b_ref, alpha):
    return _ln(alpha * x + gate * y, g_ref[...], b_ref[...])


def _ev_out_kernel(yml_ref, ysc_ref, x_ref, mod_ref, w_ref, g_ref, b_ref, o_ref, *, alpha):
    for r in range(0, x_ref.shape[0], OUT_ROWS):
        rows = slice(r, r + OUT_ROWS)
        y = _dot(yml_ref[rows, :], w_ref[0, 0:ML_VW, :]) + _dot(ysc_ref[rows, :], w_ref[0, ML_VW:, :])
        o_ref[rows, :] = _residual_ln(x_ref[rows, :], mod_ref[0, 2:3, :], y, g_ref, b_ref, alpha)


def _ev_out(st, yml, ysc, x, mod, layer, w_out, ln_g, ln_b, alpha):
    tm = st.tm
    tok = lambda w: pl.BlockSpec((tm, w), lambda i: (i, 0))
    return pl.pallas_call(
        functools.partial(_ev_out_kernel, alpha=alpha),
        grid=(st.n_tok // tm,),
        in_specs=[tok(ML_VW), tok(SC_W), tok(D_MODEL), st.mod_spec(tm), _resident_layer(w_out, layer),
                  _resident(ln_g.shape), _resident(ln_b.shape)],
        out_specs=tok(D_MODEL),
        out_shape=jax.ShapeDtypeStruct((st.n_tok, D_MODEL), F32),
        compiler_params=_cparams(("parallel",)),
        name="ev_out",
    )(yml, ysc, x, mod, w_out, ln_g, ln_b)


def _ffn_kernel(x_ref, mod_ref, wa_ref, wg_ref, ca_ref, cg_ref, wdn_ref, g_ref, b_ref, o_ref,
                h_scr, acc, *, row_w, alpha):
    j = pl.program_id(1)
    last = pl.num_programs(1) - 1
    tm = h_scr.shape[0]
    rb = max(row_w, FFN_ROWS)
    blocks = [slice(r, r + rb) for r in range(0, tm, rb)]

    def up_gate(first):
        acts = []
        for rows in blocks:
            if first:
                h = _ada(x_ref[rows, :], mod_ref, 3).astype(BF16)
                h_scr[rows, :] = h
            else:
                h = h_scr[rows, :]
            sg = _silu(_conv3(_dot(h, wg_ref[0]), cg_ref[0], row_w))
            a = _conv3(_dot(h, wa_ref[0]), ca_ref[0], row_w)
            acts.append((sg * a).astype(BF16))
        return acts

    @pl.when(j == 0)
    def _():
        acc[...] = _dot(jnp.concatenate(up_gate(True), axis=0), wdn_ref[0])

    @pl.when((j > 0) & (j < last))
    def _():
        acc[...] += _dot(jnp.concatenate(up_gate(False), axis=0), wdn_ref[0])

    @pl.when(j == last)
    def _():
        for rows, act in zip(blocks, up_gate(False)):
            y = acc[rows, :] + _dot(act, wdn_ref[0])
            o_ref[rows, :] = _residual_ln(x_ref[rows, :], mod_ref[0, 5:6, :], y, g_ref, b_ref, alpha)


def _ffn(st, x, mod, layer, w_up, c_up, w_down, ln_g, ln_b, alpha):
    tm, fc = st.tm, FFN_CHUNK
    nf = D_FF // fc
    return pl.pallas_call(
        functools.partial(_ffn_kernel, row_w=st.row_w, alpha=alpha),
        grid=(st.n_tok // tm, nf),
        in_specs=[
            pl.BlockSpec((tm, D_MODEL), lambda i, j: (i, 0)),
            st.mod_spec(tm),
            pl.BlockSpec((1, D_MODEL, fc), lambda i, j: (layer, 0, j)),
            pl.BlockSpec((1, D_MODEL, fc), lambda i, j: (layer, 0, nf + j)),
            pl.BlockSpec((1, 3, fc), lambda i, j: (layer, 0, j)),
            pl.BlockSpec((1, 3, fc), lambda i, j: (layer, 0, nf + j)),
            pl.BlockSpec((1, fc, D_MODEL), lambda i, j: (layer, j, 0)),
            pl.BlockSpec((1, D_MODEL), lambda i, j: (0, 0)),
            pl.BlockSpec((1, D_MODEL), lambda i, j: (0, 0)),
        ],
        out_specs=pl.BlockSpec((tm, D_MODEL), lambda i, j: (i, 0)),
        out_shape=jax.ShapeDtypeStruct((st.n_tok, D_MODEL), F32),
        scratch_shapes=[pltpu.VMEM((tm, D_MODEL), BF16), pltpu.VMEM((tm, D_MODEL), F32)],
        compiler_params=_cparams(("parallel", "arbitrary")),
        name="ffn",
    )(x, mod, w_up, w_up, c_up, c_up, w_down, ln_g, ln_b)


def _od_in_kernel(x_ref, mod_ref, w_ref, sg_ref, sb_ref, ws_ref, bs_ref, ygm_ref, glu_ref):
    gw = GM_W // GM_GROUPS
    hb = _ada(x_ref[...], mod_ref, 0).astype(BF16)
    part = lambda p: w_ref[0, :, p * GM_W:(p + 1) * GM_W]
    glu_ref[...] = (_dot(hb, part(2)) * jax.nn.sigmoid(_dot(hb, part(3)))).astype(BF16)
    vn = _ln(_gelu(_dot(hb, part(1))), sg_ref[...], sb_ref[...]).astype(BF16)
    u = _gelu(_dot(hb, part(0)))
    for ci in range(x_ref.shape[0] // GM_CHUNK):
        rows = slice(ci * GM_CHUNK, (ci + 1) * GM_CHUNK)
        for g in range(GM_GROUPS):
            cols = slice(g * gw, (g + 1) * gw)
            mixed = _dot(ws_ref[g], vn[rows, cols]) + bs_ref[:, g:g + 1]
            ygm_ref[rows, cols] = (u[rows, cols] * mixed).astype(BF16)


def _od_in(st, x, mod, layer, w_in, sgu_g, sgu_b, w_s, b_s_t):
    tm = st.tm_in
    tok = lambda w: pl.BlockSpec((tm, w), lambda i: (i, 0))
    out = jax.ShapeDtypeStruct((st.n_tok, GM_W), BF16)
    consts = (sgu_g, sgu_b, w_s, b_s_t)
    return pl.pallas_call(
        _od_in_kernel,
        grid=(st.n_tok // tm,),
        in_specs=[tok(D_MODEL), st.mod_spec(tm), _resident_layer(w_in, layer)] + [_resident(a.shape) for a in consts],
        out_specs=[tok(GM_W), tok(CF_W)],
        out_shape=[out, out],
        compiler_params=_cparams(("parallel",)),
        name="od_in",
    )(x, mod, w_in, *consts)


def _conv_long_kernel(x_ref, w_ref, o_ref, pad, *, stride, tb):
    half = (CF_CONV // 2) * stride
    cb = x_ref.shape[1]
    pad[0:half, :] = jnp.zeros((half, cb), F32)
    pad[half + tb:, :] = jnp.zeros((half, cb), F32)
    pad[half:half + tb, :] = x_ref[...].astype(F32)
    w = w_ref[...]

    def block(r0, aligned):
        acc = jnp.zeros((CONV_ROWS, cb), F32)
        for j in range(CF_CONV):
            start = r0 + j * stride
            acc = acc + w[j:j + 1, :] * pad[pl.ds(pl.multiple_of(start, 8) if aligned else start, CONV_ROWS), :]
        o_ref[pl.ds(r0, CONV_ROWS), :] = acc.astype(o_ref.dtype)

    n_blocks = tb // CONV_ROWS
    if stride % 8 == 0:
        def body(i, carry):
            block(pl.multiple_of(i * CONV_ROWS, CONV_ROWS), True)
            return carry
        lax.fori_loop(0, n_blocks, body, 0)
    else:
        for i in range(n_blocks):
            block(i * CONV_ROWS, False)


def _conv_long(st, x, w):
    tb, cb = st.t_batch, 256
    stride = st.long_stride
    return pl.pallas_call(
        functools.partial(_conv_long_kernel, stride=stride, tb=tb),
        grid=(st.n_batch, CF_W // cb),
        in_specs=[
            pl.BlockSpec((tb, cb), lambda b, j: (b, j)),
            pl.BlockSpec((CF_CONV, cb), lambda b, j: (0, j)),
        ],
        out_specs=pl.BlockSpec((tb, cb), lambda b, j: (b, j)),
        out_shape=jax.ShapeDtypeStruct((st.n_tok, CF_W), BF16),
        scratch_shapes=[pltpu.VMEM((tb + 2 * (CF_CONV // 2) * stride, cb), F32)],
        compiler_params=_cparams(("parallel", "parallel")),
        name="conv_long",
    )(x, w)


def _od_out_kernel(ygm_ref, cv_ref, x_ref, mod_ref, cg_ref, cb_ref, w_ref, g_ref, b_ref, o_ref, *, alpha):
    for r in range(0, x_ref.shape[0], OUT_ROWS):
        rows = slice(r, r + OUT_ROWS)
        ycf = _silu(_ln(cv_ref[rows, :].astype(F32), cg_ref[...], cb_ref[...])).astype(BF16)
        y = _dot(ygm_ref[rows, :], w_ref[0, 0:GM_W, :]) + _dot(ycf, w_ref[0, GM_W:, :])
        o_ref[rows, :] = _residual_ln(x_ref[rows, :], mod_ref[0, 2:3, :], y, g_ref, b_ref, alpha)


def _od_out(st, ygm, cv, x, mod, layer, cf_g, cf_b, w_out, ln_g, ln_b, alpha):
    tm = st.tm
    tok = lambda w: pl.BlockSpec((tm, w), lambda i: (i, 0))
    return pl.pallas_call(
        functools.partial(_od_out_kernel, alpha=alpha),
        grid=(st.n_tok // tm,),
        in_specs=[tok(GM_W), tok(CF_W), tok(D_MODEL), st.mod_spec(tm), _resident(cf_g.shape), _resident(cf_b.shape),
                  _resident_layer(w_out, layer), _resident(ln_g.shape), _resident(ln_b.shape)],
        out_specs=tok(D_MODEL),
        out_shape=jax.ShapeDtypeStruct((st.n_tok, D_MODEL), F32),
        compiler_params=_cparams(("parallel",)),
        name="od_out",
    )(ygm, cv, x, mod, cf_g, cf_b, w_out, ln_g, ln_b)


def _prep_even(w_in, b_gates, conv_k, conv_q, conv_sc):
    wk = w_in[:, 0:ML_KW]
    wq = w_in[:, EV_Q0:EV_O0]
    wv = w_in[:, EV_V0:EV_G0]
    wo = w_in[:, EV_O0:EV_SC0]
    wsc = jnp.stack([w_in[:, EV_SC0 + i * SC_W:EV_SC0 + (i + 1) * SC_W] for i in range(3)])

    def per_head(g):
        lead = g.shape[:-1]
        g = g.reshape(lead + (2, 2, ML_HEADS))
        g = jnp.moveaxis(g, -1, -3).reshape(lead + (ML_HEADS, 4))
        g = jnp.pad(g, [(0, 0)] * (len(lead) + 1) + [(0, LANES - 4)])
        return g.reshape(lead + (ML_HEADS * LANES,))

    wg = per_head(w_in[:, EV_G0:EV_G0 + ML_GATES]).astype(BF16)
    bg = per_head(b_gates.reshape(1, ML_GATES))
    return wk, wq, wv, wo, wsc, wg, bg, conv_k, conv_q, conv_sc


def _zero_state(nb):
    return (jnp.zeros((nb, ML_HEADS, 2, ML_DVX, ML_DK), F32),
            jnp.full((nb, ML_HEADS, 2, 1, LANES), M_INIT, F32))


def kernel(x, c, ctx, c_ctx, w_mod, b_mod, ln1_g, ln1_b, ln2_g, ln2_b, ffn_w_up, ffn_conv, ffn_w_down,
           ev_w_in, ev_b_gates, ev_conv_k, ev_conv_q, ev_norm_g, ev_conv_sc, ev_w_out,
           od_w_in, od_sgu_g, od_sgu_b, od_w_s, od_b_s, od_conv_dw, od_cf_g, od_cf_b, od_w_out):
    nb, seq, d = x.shape
    ctx_len = ctx.shape[1]
    depth = w_mod.shape[0]
    assert d == D_MODEL and seq % TM_LATENT == 0 and ctx_len % ML_CHUNK == 0 and nb + 1 <= 8
    alpha = (2.0 * depth) ** 0.25
    last_even = ((depth - 1) // 2) * 2

    lat = _Stream(nb, seq, TM_LATENT, TM_IN, GRID_W, GRID_W, None)
    cst = _Stream(nb, ctx_len, ctx_len, ctx_len, ctx_len, 1, nb)

    c8 = jnp.zeros((8, d), F32).at[:nb].set(c).at[nb].set(c_ctx)
    mod_all = _modulation(c8, w_mod, b_mod).reshape(depth, 8, 6, d)

    xs = x.reshape(nb * seq, d)
    cs = ctx.reshape(nb * ctx_len, d)
    row = lambda v: v.reshape(1, -1)

    ffn_up_b, ffn_down_b = ffn_w_up.astype(BF16), ffn_w_down.astype(BF16)
    ev_in_b, ev_out_b = ev_w_in.astype(BF16), ev_w_out.astype(BF16)
    od_in_b, od_out_b, od_ws_b = od_w_in.astype(BF16), od_w_out.astype(BF16), od_w_s.astype(BF16)

    for l in range(depth):
        ctx_out = l < last_even
        j = l // 2
        mod = mod_all[l]
        g1, b1, g2, b2 = row(ln1_g[l]), row(ln1_b[l]), row(ln2_g[l]), row(ln2_b[l])

        if l % 2 == 0:
            wts = _prep_even(ev_in_b[j], ev_b_gates[j], ev_conv_k[j], ev_conv_q[j], ev_conv_sc[j])
            norm_g = row(ev_norm_g[j])

            def even_mixer(st, z, state0, want_y):
                k, qt, vt, so, ysc, g, gt = _ev_in(st, z, mod, wts)
                yml, fin = _mlstm(st, k, qt, vt, so, g, gt, state0, norm_g)
                if not want_y:
                    return None, fin
                return _ev_out(st, yml, ysc, z, mod, j, ev_out_b, g1, b1, alpha), fin

            c1, fin = even_mixer(cst, cs, _zero_state(nb), ctx_out)
            x1, _ = even_mixer(lat, xs, fin, True)
        else:
            b_s_t = od_b_s[j].T

            def odd_mixer(st, z):
                ygm, glu = _od_in(st, z, mod, j, od_in_b, row(od_sgu_g[j]), row(od_sgu_b[j]), od_ws_b[j], b_s_t)
                cv = _conv_long(st, glu, od_conv_dw[j])
                return _od_out(st, ygm, cv, z, mod, j, row(od_cf_g[j]), row(od_cf_b[j]), od_out_b, g1, b1, alpha)

            x1 = odd_mixer(lat, xs)
            c1 = odd_mixer(cst, cs) if ctx_out else None

        xs = _ffn(lat, x1, mod, l, ffn_up_b, ffn_conv, ffn_down_b, g2, b2, alpha)
        if ctx_out:
            cs = _ffn(cst, c1, mod, l, ffn_up_b, ffn_conv, ffn_down_b, g2, b2, alpha)

    return xs.reshape(nb, seq, d)
```

```python
import functools

import jax
import jax.numpy as jnp
from jax import lax
from jax.experimental import pallas as pl
from jax.experimental.pallas import tpu as pltpu

F32 = jnp.float32
BF16 = jnp.bfloat16

D_MODEL = 2048
GRID_W = 64
ML_HEADS = 4
ML_DV = 256
ML_DK = 128
ML_VW = ML_HEADS * ML_DV
ML_KW = ML_HEADS * ML_DK
ML_GATES = 16
ML_CHUNK = 128
M_INIT = -1e30
SC_W = 1024
GM_W = 1024
GM_GROUPS = 4
GM_CHUNK = 128
CF_W = 1024
CF_CONV = 31
D_FF = 5632
LN_EPS = 1e-5

EV_V0 = ML_KW
EV_G0 = EV_V0 + ML_VW
EV_Q0 = EV_G0 + ML_GATES
EV_O0 = EV_Q0 + ML_KW
EV_SC0 = EV_O0 + ML_VW

V7X_VMEM_BYTES = 64 * 2**20
VMEM_LIMIT = V7X_VMEM_BYTES - 8 * 2**20
LANES = 128
BF16_ROWS = 16
ML_UNROLL = 8

TM_LATENT = 512
TM_IN = 256
FFN_CHUNK = 512
OUT_ROWS = 256
FFN_ROWS = 256
CONV_ROWS = 64
ML_DVX = ML_DV + BF16_ROWS


def _cparams(sem):
    return pltpu.CompilerParams(dimension_semantics=sem, vmem_limit_bytes=VMEM_LIMIT)


def _silu(z):
    return z * jax.nn.sigmoid(z)


def _gelu(z):
    return 0.5 * z * (1.0 + lax.erf(z * (2.0 ** -0.5)))


def _log_sigmoid(z):
    return jnp.minimum(z, 0.0) - jnp.log1p(jnp.exp(-jnp.abs(z)))


def _ln(z, g, b):
    mu = jnp.mean(z, axis=-1, keepdims=True)
    zc = z - mu
    var = jnp.mean(zc * zc, axis=-1, keepdims=True)
    return zc * lax.rsqrt(var + LN_EPS) * g + b


def _ada(x, mod_ref, slot):
    return x * (1.0 + mod_ref[0, slot + 1:slot + 2, :]) + mod_ref[0, slot:slot + 1, :]


def _conv3(a, w, row_w):
    tm = a.shape[0]
    t = lax.broadcasted_iota(jnp.int32, (tm, 1), 0) % row_w
    prev = jnp.where(t == 0, 0.0, pltpu.roll(a, 1, 0))
    nxt = jnp.where(t == row_w - 1, 0.0, pltpu.roll(a, tm - 1, 0))
    return w[0:1, :] * prev + w[1:2, :] * a + w[2:3, :] * nxt


def _dot(a, b):
    return jnp.dot(a, b, preferred_element_type=F32)


def _tri_sum(tri, z):
    hi = z.astype(BF16)
    r1 = z - hi.astype(F32)
    mid = r1.astype(BF16)
    lo = (r1 - mid.astype(F32)).astype(BF16)
    return _dot(tri, hi) + _dot(tri, mid) + _dot(tri, lo)


def _resident(shape):
    zeros = (0,) * len(shape)
    return pl.BlockSpec(shape, lambda *_: zeros, pipeline_mode=pl.Buffered(1))


def _resident_layer(stacked, layer):
    tail = (0,) * (stacked.ndim - 1)
    return pl.BlockSpec((1,) + stacked.shape[1:], lambda *_: (layer,) + tail, pipeline_mode=pl.Buffered(1))


def _mod_kernel(c_ref, w_ref, b_ref, o_ref):
    s = _silu(c_ref[...]).astype(BF16)
    o_ref[0] = _dot(s, w_ref[0].astype(BF16)) + b_ref[0]


def _modulation(c8, w_mod, b_mod):
    depth, d, n = w_mod.shape
    tn = 1024
    return pl.pallas_call(
        _mod_kernel,
        grid=(depth, n // tn),
        in_specs=[
            pl.BlockSpec((8, d), lambda l, j: (0, 0)),
            pl.BlockSpec((1, d, tn), lambda l, j: (l, 0, j)),
            pl.BlockSpec((1, 1, tn), lambda l, j: (l, 0, j)),
        ],
        out_specs=pl.BlockSpec((1, 8, tn), lambda l, j: (l, 0, j)),
        out_shape=jax.ShapeDtypeStruct((depth, 8, n), F32),
        compiler_params=_cparams(("parallel", "parallel")),
        name="modulation",
    )(c8, w_mod, b_mod.reshape(depth, 1, n))


class _Stream:
    def __init__(self, n_batch, t_batch, tm, tm_in, row_w, long_stride, mod_row):
        self.n_batch = n_batch
        self.t_batch = t_batch
        self.tm = tm
        self.tm_in = tm_in
        self.row_w = row_w
        self.long_stride = long_stride
        self.mod_row = mod_row
        self.n_tok = n_batch * t_batch

    def mod_spec(self, tm):
        if self.mod_row is None:
            tpb = self.t_batch // tm
            return pl.BlockSpec((1, 6, D_MODEL), lambda i, *_: (i // tpb, 0, 0))
        r = self.mod_row
        return pl.BlockSpec((1, 6, D_MODEL), lambda i, *_: (r, 0, 0))


def _ev_in_kernel(x_ref, mod_ref, wk_ref, wq_ref, wv_ref, wo_ref, wsc_ref, wg_ref, bg_ref, ck_ref, cq_ref, csc_ref,
                  k_ref, qt_ref, vt_ref, so_ref, ysc_ref, g_ref, gt_ref, *, row_w):
    L = ML_CHUNK
    tm = x_ref.shape[0]
    chunks = [slice(ci * L, (ci + 1) * L) for ci in range(tm // L)]
    hb = _ada(x_ref[...], mod_ref, 0).astype(BF16)

    k = _silu(_conv3(_dot(hb, wk_ref[...]), ck_ref[...], row_w)) * (ML_DK ** -0.5)
    k_ref[...] = k.astype(BF16)

    q = _silu(_conv3(_dot(hb, wq_ref[...]), cq_ref[...], row_w))
    for ci, rows in enumerate(chunks):
        for f in range(ML_KW // L):
            qt_ref[ci, f * L:(f + 1) * L, :] = q[rows, f * L:(f + 1) * L].T.astype(BF16)

    v = _dot(hb, wv_ref[...])
    for ci, rows in enumerate(chunks):
        for f in range(ML_VW // L):
            vt_ref[ci, f * L:(f + 1) * L, :] = v[rows, f * L:(f + 1) * L].T.astype(BF16)

    so_ref[...] = jax.nn.sigmoid(_dot(hb, wo_ref[...])).astype(BF16)

    gate_c = _dot(hb, wsc_ref[0])
    val = _dot(hb, wsc_ref[2])
    mixed = _conv3(gate_c * val, csc_ref[...], row_w)
    ysc_ref[...] = (_dot(hb, wsc_ref[1]) * mixed).astype(BF16)

    g = _dot(hb, wg_ref[...]) + bg_ref[...]
    lane = lax.broadcasted_iota(jnp.int32, (L, g.shape[1]), 1) % LANES
    r_i = lax.broadcasted_iota(jnp.int32, (L, L), 0)
    c_i = lax.broadcasted_iota(jnp.int32, (L, L), 1)
    lower = (c_i <= r_i).astype(BF16)
    upper = (c_i >= r_i).astype(BF16)
    for ci, rows in enumerate(chunks):
        gc = g[rows, :]
        lf = _log_sigmoid(gc)
        out = jnp.where(lane == 1, _tri_sum(lower, lf), jnp.where(lane == 3, _tri_sum(upper, lf), gc))
        g_ref[rows, :] = out
        for hd in range(ML_HEADS):
            gt_ref[ci, hd] = out[:, hd * LANES:(hd + 1) * LANES].T[0:8, :]


def _ev_in(st, x, mod, wts):
    wk, wq, wv, wo, wsc, wg, bg, ck, cq, csc = wts
    tm = st.tm_in
    nck = tm // ML_CHUNK
    n_chunks = st.n_tok // ML_CHUNK
    tok = lambda w: pl.BlockSpec((tm, w), lambda i: (i, 0))
    return pl.pallas_call(
        functools.partial(_ev_in_kernel, row_w=st.row_w),
        grid=(st.n_tok // tm,),
        in_specs=[tok(D_MODEL), st.mod_spec(tm)] + [_resident(w.shape) for w in wts],
        out_specs=[
            tok(ML_KW),
            pl.BlockSpec((nck, ML_KW, ML_CHUNK), lambda i: (i, 0, 0)),
            pl.BlockSpec((nck, ML_VW, ML_CHUNK), lambda i: (i, 0, 0)),
            tok(ML_VW),
            tok(SC_W),
            tok(ML_HEADS * LANES),
            pl.BlockSpec((nck, ML_HEADS, 8, ML_CHUNK), lambda i: (i, 0, 0, 0)),
        ],
        out_shape=[
            jax.ShapeDtypeStruct((st.n_tok, ML_KW), BF16),
            jax.ShapeDtypeStruct((n_chunks, ML_KW, ML_CHUNK), BF16),
            jax.ShapeDtypeStruct((n_chunks, ML_VW, ML_CHUNK), BF16),
            jax.ShapeDtypeStruct((st.n_tok, ML_VW), BF16),
            jax.ShapeDtypeStruct((st.n_tok, SC_W), BF16),
            jax.ShapeDtypeStruct((st.n_tok, ML_HEADS * LANES), F32),
            jax.ShapeDtypeStruct((n_chunks, ML_HEADS, 8, ML_CHUNK), F32),
        ],
        compiler_params=_cparams(("parallel",)),
        name="ev_in",
    )(x, mod, *wts)


def _mlstm_kernel(k_ref, qt_ref, vt_ref, so_ref, g_ref, gt_ref, c0_ref, m0_ref, ng_ref,
                  y_ref, cf_ref, mf_ref, st_scr, nm_scr, *, nch):
    L = ML_CHUNK
    s_i = lax.broadcasted_iota(jnp.int32, (L, L), 0)
    t_i = lax.broadcasted_iota(jnp.int32, (L, L), 1)
    row_8 = lax.broadcasted_iota(jnp.int32, (8, LANES), 0)

    def gate_rows(g8, d):
        ig_row, b_row = g8[2 * d:2 * d + 1, :], g8[2 * d + 1:2 * d + 2, :]
        b_last = b_row[:, L - 1:L] if d == 0 else b_row[:, 0:1]
        return ig_row, b_row, b_last

    def local_pass(c, carry):
        r = pl.multiple_of(c * L, L)
        g8 = gt_ref[c, 0]
        kc = k_ref[pl.ds(r, L), :]
        vtc = vt_ref[c].astype(F32)
        for d in range(2):
            ig_row, b_row, b_last = gate_rows(g8, d)
            a_row = b_last - b_row + ig_row
            m_loc = jnp.max(a_row, axis=-1, keepdims=True)
            w_row = jnp.exp(a_row - m_loc)
            lhs = jnp.concatenate([vtc * w_row, jnp.broadcast_to(w_row, (BF16_ROWS, L))], axis=0)
            st_scr[d, c] = _dot(lhs.astype(BF16), kc)
            nm_scr[d, c] = jnp.where(row_8 == 0, m_loc, b_last)
        return carry

    lax.fori_loop(0, nch, local_pass, 0, unroll=min(ML_UNROLL, nch))

    for d in range(2):
        def scan_pass(it, carry, d=d):
            cx, m_prev = carry
            c = it if d == 0 else nch - 1 - it
            loc = st_scr[d, c]
            blk = nm_scr[d, c]
            m_loc, b_last = blk[0:1, 0:1], blk[1:2, 0:1]
            st_scr[d, c] = cx
            nm_scr[d, c] = jnp.broadcast_to(m_prev, (8, LANES))
            m_new = jnp.maximum(b_last + m_prev, m_loc)
            return jnp.exp(b_last + m_prev - m_new) * cx + jnp.exp(m_loc - m_new) * loc, m_new

        cx, m_fin = lax.fori_loop(0, nch, scan_pass, (c0_ref[0, 0, d], m0_ref[0, 0, d][:, 0:1]))
        cf_ref[0, 0, d] = cx
        mf_ref[0, 0, d] = jnp.broadcast_to(m_fin, (1, LANES))

    def output_pass(c, carry):
        r = pl.multiple_of(c * L, L)
        g8 = gt_ref[c, 0]
        gch = g_ref[pl.ds(r, L), :]
        qtc = qt_ref[c]
        vtc = vt_ref[c]
        sraw = _dot(k_ref[pl.ds(r, L), :], qtc)
        h = None
        for d in range(2):
            ig_row, b_row, _ = gate_rows(g8, d)
            src_col = gch[:, 2 * d:2 * d + 1] - gch[:, 2 * d + 1:2 * d + 2]
            visible = (s_i <= t_i) if d == 0 else (s_i >= t_i)
            log_intra = jnp.where(visible, src_col + b_row, -jnp.inf)
            m_prev = nm_scr[d, c][0:1, 0:1]
            log_inter = b_row + m_prev
            m_t = jnp.maximum(log_inter, jnp.max(log_intra, axis=0, keepdims=True))
            s = sraw * jnp.exp(log_intra - m_t)
            w_inter = jnp.exp(log_inter - m_t)
            inter = _dot(st_scr[d, c].astype(BF16), qtc)
            num = _dot(vtc, s.astype(BF16)) + w_inter * inter[0:ML_DV, :]
            den = jnp.sum(s, axis=0, keepdims=True) + w_inter * inter[ML_DV:ML_DV + 1, :]
            hd = num * (1.0 / jnp.maximum(jnp.abs(den), jnp.exp(-m_t)))
            h = hd if h is None else h + hd
        mu = jnp.mean(h, axis=0, keepdims=True)
        hc = h - mu
        var = jnp.mean(hc * hc, axis=0, keepdims=True)
        hn = (hc * lax.rsqrt(var + LN_EPS)).T
        y_ref[pl.ds(r, L), :] = (hn * ng_ref[...] * so_ref[pl.ds(r, L), :].astype(F32)).astype(BF16)
        return carry

    lax.fori_loop(0, nch, output_pass, 0, unroll=min(ML_UNROLL, nch))


def _mlstm(st, k, qt, vt, so, g, gt, state0, norm_g):
    tb = st.t_batch
    nch = tb // ML_CHUNK
    nb, nh = st.n_batch, ML_HEADS
    c0, m0 = state0
    state_specs = [
        pl.BlockSpec((1, 1, 2, ML_DVX, ML_DK), lambda b, h: (b, h, 0, 0, 0)),
        pl.BlockSpec((1, 1, 2, 1, LANES), lambda b, h: (b, h, 0, 0, 0)),
    ]
    y, cf, mf = pl.pallas_call(
        functools.partial(_mlstm_kernel, nch=nch),
        grid=(nb, nh),
        in_specs=[
            pl.BlockSpec((tb, ML_DK), lambda b, h: (b, h)),
            pl.BlockSpec((nch, ML_DK, ML_CHUNK), lambda b, h: (b, h, 0)),
            pl.BlockSpec((nch, ML_DV, ML_CHUNK), lambda b, h: (b, h, 0)),
            pl.BlockSpec((tb, ML_DV), lambda b, h: (b, h)),
            pl.BlockSpec((tb, LANES), lambda b, h: (b, h)),
            pl.BlockSpec((nch, 1, 8, ML_CHUNK), lambda b, h: (b, h, 0, 0)),
        ] + state_specs + [pl.BlockSpec((1, ML_DV), lambda b, h: (0, h))],
        out_specs=[pl.BlockSpec((tb, ML_DV), lambda b, h: (b, h))] + state_specs,
        out_shape=[
            jax.ShapeDtypeStruct((st.n_tok, ML_VW), BF16),
            jax.ShapeDtypeStruct((nb, nh, 2, ML_DVX, ML_DK), F32),
            jax.ShapeDtypeStruct((nb, nh, 2, 1, LANES), F32),
        ],
        scratch_shapes=[
            pltpu.VMEM((2, nch, ML_DVX, ML_DK), F32),
            pltpu.VMEM((2, nch, 8, LANES), F32),
        ],
        compiler_params=_cparams(("parallel", "parallel")),
        name="mlstm",
    )(k, qt, vt, so, g, gt, c0, m0, norm_g)
    return y, (cf, mf)


def _residual_ln(x, gate, y, g_ref, b_ref, alpha):
    return _ln(alpha * x + gate * y, g_ref[...], b_ref[...])


def _ev_out_kernel(yml_ref, ysc_ref, x_ref, mod_ref, w_ref, g_ref, b_ref, o_ref, *, alpha):
    for r in range(0, x_ref.shape[0], OUT_ROWS):
        rows = slice(r, r + OUT_ROWS)
        y = _dot(yml_ref[rows, :], w_ref[0, 0:ML_VW, :]) + _dot(ysc_ref[rows, :], w_ref[0, ML_VW:, :])
        o_ref[rows, :] = _residual_ln(x_ref[rows, :], mod_ref[0, 2:3, :], y, g_ref, b_ref, alpha)


def _ev_out(st, yml, ysc, x, mod, layer, w_out, ln_g, ln_b, alpha):
    tm = st.tm
    tok = lambda w: pl.BlockSpec((tm, w), lambda i: (i, 0))
    return pl.pallas_call(
        functools.partial(_ev_out_kernel, alpha=alpha),
        grid=(st.n_tok // tm,),
        in_specs=[tok(ML_VW), tok(SC_W), tok(D_MODEL), st.mod_spec(tm), _resident_layer(w_out, layer),
                  _resident(ln_g.shape), _resident(ln_b.shape)],
        out_specs=tok(D_MODEL),
        out_shape=jax.ShapeDtypeStruct((st.n_tok, D_MODEL), F32),
        compiler_params=_cparams(("parallel",)),
        name="ev_out",
    )(yml, ysc, x, mod, w_out, ln_g, ln_b)


def _ffn_kernel(x_ref, mod_ref, wa_ref, wg_ref, ca_ref, cg_ref, wdn_ref, g_ref, b_ref, o_ref,
                h_scr, acc, *, row_w, alpha):
    j = pl.program_id(1)
    last = pl.num_programs(1) - 1
    tm = h_scr.shape[0]
    rb = max(row_w, FFN_ROWS)
    blocks = [slice(r, r + rb) for r in range(0, tm, rb)]

    def up_gate(first):
        acts = []
        for rows in blocks:
            if first:
                h = _ada(x_ref[rows, :], mod_ref, 3).astype(BF16)
                h_scr[rows, :] = h
            else:
                h = h_scr[rows, :]
            sg = _silu(_conv3(_dot(h, wg_ref[0]), cg_ref[0], row_w))
            a = _conv3(_dot(h, wa_ref[0]), ca_ref[0], row_w)
            acts.append((sg * a).astype(BF16))
        return acts

    @pl.when(j == 0)
    def _():
        acc[...] = _dot(jnp.concatenate(up_gate(True), axis=0), wdn_ref[0])

    @pl.when((j > 0) & (j < last))
    def _():
        acc[...] += _dot(jnp.concatenate(up_gate(False), axis=0), wdn_ref[0])

    @pl.when(j == last)
    def _():
        for rows, act in zip(blocks, up_gate(False)):
            y = acc[rows, :] + _dot(act, wdn_ref[0])
            o_ref[rows, :] = _residual_ln(x_ref[rows, :], mod_ref[0, 5:6, :], y, g_ref, b_ref, alpha)


def _ffn(st, x, mod, layer, w_up, c_up, w_down, ln_g, ln_b, alpha):
    tm, fc = st.tm, FFN_CHUNK
    nf = D_FF // fc
    return pl.pallas_call(
        functools.partial(_ffn_kernel, row_w=st.row_w, alpha=alpha),
        grid=(st.n_tok // tm, nf),
        in_specs=[
            pl.BlockSpec((tm, D_MODEL), lambda i, j: (i, 0)),
            st.mod_spec(tm),
            pl.BlockSpec((1, D_MODEL, fc), lambda i, j: (layer, 0, j)),
            pl.BlockSpec((1, D_MODEL, fc), lambda i, j: (layer, 0, nf + j)),
            pl.BlockSpec((1, 3, fc), lambda i, j: (layer, 0, j)),
            pl.BlockSpec((1, 3, fc), lambda i, j: (layer, 0, nf + j)),
            pl.BlockSpec((1, fc, D_MODEL), lambda i, j: (layer, j, 0)),
            pl.BlockSpec((1, D_MODEL), lambda i, j: (0, 0)),
            pl.BlockSpec((1, D_MODEL), lambda i, j: (0, 0)),
        ],
        out_specs=pl.BlockSpec((tm, D_MODEL), lambda i, j: (i, 0)),
        out_shape=jax.ShapeDtypeStruct((st.n_tok, D_MODEL), F32),
        scratch_shapes=[pltpu.VMEM((tm, D_MODEL), BF16), pltpu.VMEM((tm, D_MODEL), F32)],
        compiler_params=_cparams(("parallel", "arbitrary")),
        name="ffn",
    )(x, mod, w_up, w_up, c_up, c_up, w_down, ln_g, ln_b)


def _od_in_kernel(x_ref, mod_ref, w_ref, sg_ref, sb_ref, ws_ref, bs_ref, ygm_ref, glu_ref):
    gw = GM_W // GM_GROUPS
    hb = _ada(x_ref[...], mod_ref, 0).astype(BF16)
    part = lambda p: w_ref[0, :, p * GM_W:(p + 1) * GM_W]
    glu_ref[...] = (_dot(hb, part(2)) * jax.nn.sigmoid(_dot(hb, part(3)))).astype(BF16)
    vn = _ln(_gelu(_dot(hb, part(1))), sg_ref[...], sb_ref[...]).astype(BF16)
    u = _gelu(_dot(hb, part(0)))
    for ci in range(x_ref.shape[0] // GM_CHUNK):
        rows = slice(ci * GM_CHUNK, (ci + 1) * GM_CHUNK)
        for g in range(GM_GROUPS):
            cols = slice(g * gw, (g + 1) * gw)
            mixed = _dot(ws_ref[g], vn[rows, cols]) + bs_ref[:, g:g + 1]
            ygm_ref[rows, cols] = (u[rows, cols] * mixed).astype(BF16)


def _od_in(st, x, mod, layer, w_in, sgu_g, sgu_b, w_s, b_s_t):
    tm = st.tm_in
    tok = lambda w: pl.BlockSpec((tm, w), lambda i: (i, 0))
    out = jax.ShapeDtypeStruct((st.n_tok, GM_W), BF16)
    consts = (sgu_g, sgu_b, w_s, b_s_t)
    return pl.pallas_call(
        _od_in_kernel,
        grid=(st.n_tok // tm,),
        in_specs=[tok(D_MODEL), st.mod_spec(tm), _resident_layer(w_in, layer)] + [_resident(a.shape) for a in consts],
        out_specs=[tok(GM_W), tok(CF_W)],
        out_shape=[out, out],
        compiler_params=_cparams(("parallel",)),
        name="od_in",
    )(x, mod, w_in, *consts)


def _conv_long_kernel(x_ref, w_ref, o_ref, pad, *, stride, tb):
    half = (CF_CONV // 2) * stride
    cb = x_ref.shape[1]
    pad[0:half, :] = jnp.zeros((half, cb), F32)
    pad[half + tb:, :] = jnp.zeros((half, cb), F32)
    pad[half:half + tb, :] = x_ref[...].astype(F32)
    w = w_ref[...]

    def block(r0, aligned):
        acc = jnp.zeros((CONV_ROWS, cb), F32)
        for j in range(CF_CONV):
            start = r0 + j * stride
            acc = acc + w[j:j + 1, :] * pad[pl.ds(pl.multiple_of(start, 8) if aligned else start, CONV_ROWS), :]
        o_ref[pl.ds(r0, CONV_ROWS), :] = acc.astype(o_ref.dtype)

    n_blocks = tb // CONV_ROWS
    if stride % 8 == 0:
        def body(i, carry):
            block(pl.multiple_of(i * CONV_ROWS, CONV_ROWS), True)
            return carry
        lax.fori_loop(0, n_blocks, body, 0)
    else:
        for i in range(n_blocks):
            block(i * CONV_ROWS, False)


def _conv_long(st, x, w):
    tb, cb = st.t_batch, 256
    stride = st.long_stride
    return pl.pallas_call(
        functools.partial(_conv_long_kernel, stride=stride, tb=tb),
        grid=(st.n_batch, CF_W // cb),
        in_specs=[
            pl.BlockSpec((tb, cb), lambda b, j: (b, j)),
            pl.BlockSpec((CF_CONV, cb), lambda b, j: (0, j)),
        ],
        out_specs=pl.BlockSpec((tb, cb), lambda b, j: (b, j)),
        out_shape=jax.ShapeDtypeStruct((st.n_tok, CF_W), BF16),
        scratch_shapes=[pltpu.VMEM((tb + 2 * (CF_CONV // 2) * stride, cb), F32)],
        compiler_params=_cparams(("parallel", "parallel")),
        name="conv_long",
    )(x, w)


def _od_out_kernel(ygm_ref, cv_ref, x_ref, mod_ref, cg_ref, cb_ref, w_ref, g_ref, b_ref, o_ref, *, alpha):
    for r in range(0, x_ref.shape[0], OUT_ROWS):
        rows = slice(r, r + OUT_ROWS)
        ycf = _silu(_ln(cv_ref[rows, :].astype(F32), cg_ref[...], cb_ref[...])).astype(BF16)
        y = _dot(ygm_ref[rows, :], w_ref[0, 0:GM_W, :]) + _dot(ycf, w_ref[0, GM_W:, :])
        o_ref[rows, :] = _residual_ln(x_ref[rows, :], mod_ref[0, 2:3, :], y, g_ref, b_ref, alpha)


def _od_out(st, ygm, cv, x, mod, layer, cf_g, cf_b, w_out, ln_g, ln_b, alpha):
    tm = st.tm
    tok = lambda w: pl.BlockSpec((tm, w), lambda i: (i, 0))
    return pl.pallas_call(
        functools.partial(_od_out_kernel, alpha=alpha),
        grid=(st.n_tok // tm,),
        in_specs=[tok(GM_W), tok(CF_W), tok(D_MODEL), st.mod_spec(tm), _resident(cf_g.shape), _resident(cf_b.shape),
                  _resident_layer(w_out, layer), _resident(ln_g.shape), _resident(ln_b.shape)],
        out_specs=tok(D_MODEL),
        out_shape=jax.ShapeDtypeStruct((st.n_tok, D_MODEL), F32),
        compiler_params=_cparams(("parallel",)),
        name="od_out",
    )(ygm, cv, x, mod, cf_g, cf_b, w_out, ln_g, ln_b)


def _prep_even(w_in, b_gates, conv_k, conv_q, conv_sc):
    wk = w_in[:, 0:ML_KW]
    wq = w_in[:, EV_Q0:EV_O0]
    wv = w_in[:, EV_V0:EV_G0]
    wo = w_in[:, EV_O0:EV_SC0]
    wsc = jnp.stack([w_in[:, EV_SC0 + i * SC_W:EV_SC0 + (i + 1) * SC_W] for i in range(3)])

    def per_head(g):
        lead = g.shape[:-1]
        g = g.reshape(lead + (2, 2, ML_HEADS))
        g = jnp.moveaxis(g, -1, -3).reshape(lead + (ML_HEADS, 4))
        g = jnp.pad(g, [(0, 0)] * (len(lead) + 1) + [(0, LANES - 4)])
        return g.reshape(lead + (ML_HEADS * LANES,))

    wg = per_head(w_in[:, EV_G0:EV_G0 + ML_GATES]).astype(BF16)
    bg = per_head(b_gates.reshape(1, ML_GATES))
    return wk, wq, wv, wo, wsc, wg, bg, conv_k, conv_q, conv_sc


def _zero_state(nb):
    return (jnp.zeros((nb, ML_HEADS, 2, ML_DVX, ML_DK), F32),
            jnp.full((nb, ML_HEADS, 2, 1, LANES), M_INIT, F32))


def kernel(x, c, ctx, c_ctx, w_mod, b_mod, ln1_g, ln1_b, ln2_g, ln2_b, ffn_w_up, ffn_conv, ffn_w_down,
           ev_w_in, ev_b_gates, ev_conv_k, ev_conv_q, ev_norm_g, ev_conv_sc, ev_w_out,
           od_w_in, od_sgu_g, od_sgu_b, od_w_s, od_b_s, od_conv_dw, od_cf_g, od_cf_b, od_w_out):
    nb, seq, d = x.shape
    ctx_len = ctx.shape[1]
    depth = w_mod.shape[0]
    assert d == D_MODEL and seq % TM_LATENT == 0 and ctx_len % ML_CHUNK == 0 and nb + 1 <= 8
    alpha = (2.0 * depth) ** 0.25
    last_even = ((depth - 1) // 2) * 2

    lat = _Stream(nb, seq, TM_LATENT, TM_IN, GRID_W, GRID_W, None)
    cst = _Stream(nb, ctx_len, ctx_len, ctx_len, ctx_len, 1, nb)

    c8 = jnp.zeros((8, d), F32).at[:nb].set(c).at[nb].set(c_ctx)
    mod_all = _modulation(c8, w_mod, b_mod).reshape(depth, 8, 6, d)

    xs = x.reshape(nb * seq, d)
    cs = ctx.reshape(nb * ctx_len, d)
    row = lambda v: v.reshape(1, -1)

    ffn_up_b, ffn_down_b = ffn_w_up.astype(BF16), ffn_w_down.astype(BF16)
    ev_in_b, ev_out_b = ev_w_in.astype(BF16), ev_w_out.astype(BF16)
    od_in_b, od_out_b, od_ws_b = od_w_in.astype(BF16), od_w_out.astype(BF16), od_w_s.astype(BF16)

    for l in range(depth):
        ctx_out = l < last_even
        j = l // 2
        mod = mod_all[l]
        g1, b1, g2, b2 = row(ln1_g[l]), row(ln1_b[l]), row(ln2_g[l]), row(ln2_b[l])

        if l % 2 == 0:
            wts = _prep_even(ev_in_b[j], ev_b_gates[j], ev_conv_k[j], ev_conv_q[j], ev_conv_sc[j])
            norm_g = row(ev_norm_g[j])

            def even_mixer(st, z, state0, want_y):
                k, qt, vt, so, ysc, g, gt = _ev_in(st, z, mod, wts)
                yml, fin = _mlstm(st, k, qt, vt, so, g, gt, state0, norm_g)
                if not want_y:
                    return None, fin
                return _ev_out(st, yml, ysc, z, mod, j, ev_out_b, g1, b1, alpha), fin

            c1, fin = even_mixer(cst, cs, _zero_state(nb), ctx_out)
            x1, _ = even_mixer(lat, xs, fin, True)
        else:
            b_s_t = od_b_s[j].T

            def odd_mixer(st, z):
                ygm, glu = _od_in(st, z, mod, j, od_in_b, row(od_sgu_g[j]), row(od_sgu_b[j]), od_ws_b[j], b_s_t)
                cv = _conv_long(st, glu, od_conv_dw[j])
                return _od_out(st, ygm, cv, z, mod, j, row(od_cf_g[j]), row(od_cf_b[j]), od_out_b, g1, b1, alpha)

            x1 = odd_mixer(lat, xs)
            c1 = odd_mixer(cst, cs) if ctx_out else None

        xs = _ffn(lat, x1, mod, l, ffn_up_b, ffn_conv, ffn_down_b, g2, b2, alpha)
        if ctx_out:
            cs = _ffn(cst, c1, mod, l, ffn_up_b, ffn_conv, ffn_down_b, g2, b2, alpha)

    return xs.reshape(nb, seq, d)
```

```python
import functools

import jax
import jax.numpy as jnp
from jax import lax
from jax.experimental import pallas as pl
from jax.experimental.pallas import tpu as pltpu

F32 = jnp.float32
BF16 = jnp.bfloat16

D_MODEL = 2048
GRID_W = 64
ML_HEADS = 4
ML_DV = 256
ML_DK = 128
ML_VW = ML_HEADS * ML_DV
ML_KW = ML_HEADS * ML_DK
ML_GATES = 16
ML_CHUNK = 128
M_INIT = -1e30
SC_W = 1024
GM_W = 1024
GM_GROUPS = 4
GM_CHUNK = 128
CF_W = 1024
CF_CONV = 31
D_FF = 5632
LN_EPS = 1e-5

EV_V0 = ML_KW
EV_G0 = EV_V0 + ML_VW
EV_Q0 = EV_G0 + ML_GATES
EV_O0 = EV_Q0 + ML_KW
EV_SC0 = EV_O0 + ML_VW

V7X_VMEM_BYTES = 64 * 2**20
VMEM_LIMIT = V7X_VMEM_BYTES - 8 * 2**20
LANES = 128
BF16_ROWS = 16
ML_UNROLL = 8

TM_LATENT = 512
TM_IN = 256
FFN_CHUNK = 512
OUT_ROWS = 256
FFN_ROWS = 256
CONV_ROWS = 64
ML_DVX = ML_DV + BF16_ROWS


def _cparams(sem):
    return pltpu.CompilerParams(dimension_semantics=sem, vmem_limit_bytes=VMEM_LIMIT)


def _silu(z):
    return z * jax.nn.sigmoid(z)


def _gelu(z):
    return 0.5 * z * (1.0 + lax.erf(z * (2.0 ** -0.5)))


def _log_sigmoid(z):
    return jnp.minimum(z, 0.0) - jnp.log1p(jnp.exp(-jnp.abs(z)))


def _ln(z, g, b):
    mu = jnp.mean(z, axis=-1, keepdims=True)
    zc = z - mu
    var = jnp.mean(zc * zc, axis=-1, keepdims=True)
    return zc * lax.rsqrt(var + LN_EPS) * g + b


def _ada(x, mod_ref, slot):
    return x * (1.0 + mod_ref[0, slot + 1:slot + 2, :]) + mod_ref[0, slot:slot + 1, :]


def _conv3(a, w, row_w):
    tm = a.shape[0]
    t = lax.broadcasted_iota(jnp.int32, (tm, 1), 0) % row_w
    prev = jnp.where(t == 0, 0.0, pltpu.roll(a, 1, 0))
    nxt = jnp.where(t == row_w - 1, 0.0, pltpu.roll(a, tm - 1, 0))
    return w[0:1, :] * prev + w[1:2, :] * a + w[2:3, :] * nxt


def _dot(a, b):
    return jnp.dot(a, b, preferred_element_type=F32)


def _tri_sum(tri, z):
    hi = z.astype(BF16)
    r1 = z - hi.astype(F32)
    mid = r1.astype(BF16)
    lo = (r1 - mid.astype(F32)).astype(BF16)
    return _dot(tri, hi) + _dot(tri, mid) + _dot(tri, lo)


def _resident(shape):
    zeros = (0,) * len(shape)
    return pl.BlockSpec(shape, lambda *_: zeros, pipeline_mode=pl.Buffered(1))


def _resident_layer(stacked, layer):
    tail = (0,) * (stacked.ndim - 1)
    return pl.BlockSpec((1,) + stacked.shape[1:], lambda *_: (layer,) + tail, pipeline_mode=pl.Buffered(1))


def _mod_kernel(c_ref, w_ref, b_ref, o_ref):
    s = _silu(c_ref[...]).astype(BF16)
    o_ref[0] = _dot(s, w_ref[0].astype(BF16)) + b_ref[0]


def _modulation(c8, w_mod, b_mod):
    depth, d, n = w_mod.shape
    tn = 1024
    return pl.pallas_call(
        _mod_kernel,
        grid=(depth, n // tn),
        in_specs=[
            pl.BlockSpec((8, d), lambda l, j: (0, 0)),
            pl.BlockSpec((1, d, tn), lambda l, j: (l, 0, j)),
            pl.BlockSpec((1, 1, tn), lambda l, j: (l, 0, j)),
        ],
        out_specs=pl.BlockSpec((1, 8, tn), lambda l, j: (l, 0, j)),
        out_shape=jax.ShapeDtypeStruct((depth, 8, n), F32),
        compiler_params=_cparams(("parallel", "parallel")),
        name="modulation",
    )(c8, w_mod, b_mod.reshape(depth, 1, n))


class _Stream:
    def __init__(self, n_batch, t_batch, tm, tm_in, row_w, long_stride, mod_row):
        self.n_batch = n_batch
        self.t_batch = t_batch
        self.tm = tm
        self.tm_in = tm_in
        self.row_w = row_w
        self.long_stride = long_stride
        self.mod_row = mod_row
        self.n_tok = n_batch * t_batch

    def mod_spec(self, tm):
        if self.mod_row is None:
            tpb = self.t_batch // tm
            return pl.BlockSpec((1, 6, D_MODEL), lambda i, *_: (i // tpb, 0, 0))
        r = self.mod_row
        return pl.BlockSpec((1, 6, D_MODEL), lambda i, *_: (r, 0, 0))


def _ev_in_kernel(x_ref, mod_ref, wk_ref, wq_ref, wv_ref, wo_ref, wsc_ref, wg_ref, bg_ref, ck_ref, cq_ref, csc_ref,
                  k_ref, qt_ref, vt_ref, so_ref, ysc_ref, g_ref, gt_ref, *, row_w):
    L = ML_CHUNK
    tm = x_ref.shape[0]
    chunks = [slice(ci * L, (ci + 1) * L) for ci in range(tm // L)]
    hb = _ada(x_ref[...], mod_ref, 0).astype(BF16)

    k = _silu(_conv3(_dot(hb, wk_ref[...]), ck_ref[...], row_w)) * (ML_DK ** -0.5)
    k_ref[...] = k.astype(BF16)

    q = _silu(_conv3(_dot(hb, wq_ref[...]), cq_ref[...], row_w))
    for ci, rows in enumerate(chunks):
        for f in range(ML_KW // L):
            qt_ref[ci, f * L:(f + 1) * L, :] = q[rows, f * L:(f + 1) * L].T.astype(BF16)

    v = _dot(hb, wv_ref[...])
    for ci, rows in enumerate(chunks):
        for f in range(ML_VW // L):
            vt_ref[ci, f * L:(f + 1) * L, :] = v[rows, f * L:(f + 1) * L].T.astype(BF16)

    so_ref[...] = jax.nn.sigmoid(_dot(hb, wo_ref[...])).astype(BF16)

    gate_c = _dot(hb, wsc_ref[0])
    val = _dot(hb, wsc_ref[2])
    mixed = _conv3(gate_c * val, csc_ref[...], row_w)
    ysc_ref[...] = (_dot(hb, wsc_ref[1]) * mixed).astype(BF16)

    g = _dot(hb, wg_ref[...]) + bg_ref[...]
    kind = lax.broadcasted_iota(jnp.int32, (L, LANES), 1) % 4
    r_i = lax.broadcasted_iota(jnp.int32, (L, L), 0)
    c_i = lax.broadcasted_iota(jnp.int32, (L, L), 1)
    lower = (c_i <= r_i).astype(BF16)
    upper = (c_i >= r_i).astype(BF16)
    for ci, rows in enumerate(chunks):
        gc = g[rows, :]
        lf = _log_sigmoid(gc)
        out = jnp.where(kind == 1, _tri_sum(lower, lf), jnp.where(kind == 3, _tri_sum(upper, lf), gc))
        for hd in range(ML_HEADS):
            mine = out if hd == 0 else pltpu.roll(out, LANES - 4 * hd, 1)
            g_ref[rows, hd * LANES:(hd + 1) * LANES] = mine
            gt_ref[ci, hd] = mine.T[0:8, :]


def _ev_in(st, x, mod, wts):
    wk, wq, wv, wo, wsc, wg, bg, ck, cq, csc = wts
    tm = st.tm_in
    nck = tm // ML_CHUNK
    n_chunks = st.n_tok // ML_CHUNK
    tok = lambda w: pl.BlockSpec((tm, w), lambda i: (i, 0))
    return pl.pallas_call(
        functools.partial(_ev_in_kernel, row_w=st.row_w),
        grid=(st.n_tok // tm,),
        in_specs=[tok(D_MODEL), st.mod_spec(tm)] + [_resident(w.shape) for w in wts],
        out_specs=[
            tok(ML_KW),
            pl.BlockSpec((nck, ML_KW, ML_CHUNK), lambda i: (i, 0, 0)),
            pl.BlockSpec((nck, ML_VW, ML_CHUNK), lambda i: (i, 0, 0)),
            tok(ML_VW),
            tok(SC_W),
            tok(ML_HEADS * LANES),
            pl.BlockSpec((nck, ML_HEADS, 8, ML_CHUNK), lambda i: (i, 0, 0, 0)),
        ],
        out_shape=[
            jax.ShapeDtypeStruct((st.n_tok, ML_KW), BF16),
            jax.ShapeDtypeStruct((n_chunks, ML_KW, ML_CHUNK), BF16),
            jax.ShapeDtypeStruct((n_chunks, ML_VW, ML_CHUNK), BF16),
            jax.ShapeDtypeStruct((st.n_tok, ML_VW), BF16),
            jax.ShapeDtypeStruct((st.n_tok, SC_W), BF16),
            jax.ShapeDtypeStruct((st.n_tok, ML_HEADS * LANES), F32),
            jax.ShapeDtypeStruct((n_chunks, ML_HEADS, 8, ML_CHUNK), F32),
        ],
        compiler_params=_cparams(("parallel",)),
        name="ev_in",
    )(x, mod, *wts)


def _mlstm_kernel(k_ref, qt_ref, vt_ref, so_ref, g_ref, gt_ref, c0_ref, m0_ref, ng_ref,
                  y_ref, cf_ref, mf_ref, st_scr, nm_scr, *, nch):
    L = ML_CHUNK
    s_i = lax.broadcasted_iota(jnp.int32, (L, L), 0)
    t_i = lax.broadcasted_iota(jnp.int32, (L, L), 1)
    row_8 = lax.broadcasted_iota(jnp.int32, (8, LANES), 0)

    def gate_rows(g8, d):
        ig_row, b_row = g8[2 * d:2 * d + 1, :], g8[2 * d + 1:2 * d + 2, :]
        b_last = b_row[:, L - 1:L] if d == 0 else b_row[:, 0:1]
        return ig_row, b_row, b_last

    def local_pass(c, carry):
        r = pl.multiple_of(c * L, L)
        g8 = gt_ref[c, 0]
        kc = k_ref[pl.ds(r, L), :]
        vtc = vt_ref[c].astype(F32)
        for d in range(2):
            ig_row, b_row, b_last = gate_rows(g8, d)
            a_row = b_last - b_row + ig_row
            m_loc = jnp.max(a_row, axis=-1, keepdims=True)
            w_row = jnp.exp(a_row - m_loc)
            lhs = jnp.concatenate([vtc * w_row, jnp.broadcast_to(w_row, (BF16_ROWS, L))], axis=0)
            st_scr[d, c] = _dot(lhs.astype(BF16), kc)
            nm_scr[d, c] = jnp.where(row_8 == 0, m_loc, b_last)
        return carry

    lax.fori_loop(0, nch, local_pass, 0, unroll=min(ML_UNROLL, nch))

    for d in range(2):
        def stabiliser_scan(it, m_prev, d=d):
            c = it if d == 0 else nch - 1 - it
            blk = nm_scr[d, c]
            m_loc, b_last = blk[0:1, :], blk[1:2, :]
            m_new = jnp.maximum(b_last + m_prev, m_loc)
            e_prev = jnp.exp(b_last + m_prev - m_new)
            e_loc = jnp.exp(m_loc - m_new)
            nm_scr[d, c] = jnp.where(row_8 == 0, m_prev, jnp.where(row_8 == 1, e_prev, e_loc))
            return m_new

        def state_scan(it, cx, d=d):
            c = it if d == 0 else nch - 1 - it
            loc = st_scr[d, c]
            blk = nm_scr[d, c]
            st_scr[d, c] = cx
            return blk[1:2, :] * cx + blk[2:3, :] * loc

        mf_ref[0, 0, d] = lax.fori_loop(0, nch, stabiliser_scan, m0_ref[0, 0, d])
        cf_ref[0, 0, d] = lax.fori_loop(0, nch, state_scan, c0_ref[0, 0, d])

    def output_pass(c, carry):
        r = pl.multiple_of(c * L, L)
        g8 = gt_ref[c, 0]
        gch = g_ref[pl.ds(r, L), :]
        qtc = qt_ref[c]
        vtc = vt_ref[c]
        sraw = _dot(k_ref[pl.ds(r, L), :], qtc)
        h = None
        for d in range(2):
            ig_row, b_row, _ = gate_rows(g8, d)
            src_col = gch[:, 2 * d:2 * d + 1] - gch[:, 2 * d + 1:2 * d + 2]
            visible = (s_i <= t_i) if d == 0 else (s_i >= t_i)
            log_intra = jnp.where(visible, src_col + b_row, -jnp.inf)
            m_prev = nm_scr[d, c][0:1, 0:1]
            log_inter = b_row + m_prev
            m_t = jnp.maximum(log_inter, jnp.max(log_intra, axis=0, keepdims=True))
            s = sraw * jnp.exp(log_intra - m_t)
            w_inter = jnp.exp(log_inter - m_t)
            inter = _dot(st_scr[d, c].astype(BF16), qtc)
            num = _dot(vtc, s.astype(BF16)) + w_inter * inter[0:ML_DV, :]
            den = jnp.sum(s, axis=0, keepdims=True) + w_inter * inter[ML_DV:ML_DV + 1, :]
            hd = num * (1.0 / jnp.maximum(jnp.abs(den), jnp.exp(-m_t)))
            h = hd if h is None else h + hd
        mu = jnp.mean(h, axis=0, keepdims=True)
        hc = h - mu
        var = jnp.mean(hc * hc, axis=0, keepdims=True)
        hn = (hc * lax.rsqrt(var + LN_EPS)).T
        y_ref[pl.ds(r, L), :] = (hn * ng_ref[...] * so_ref[pl.ds(r, L), :].astype(F32)).astype(BF16)
        return carry

    lax.fori_loop(0, nch, output_pass, 0, unroll=min(ML_UNROLL, nch))


def _mlstm(st, k, qt, vt, so, g, gt, state0, norm_g):
    tb = st.t_batch
    nch = tb // ML_CHUNK
    nb, nh = st.n_batch, ML_HEADS
    c0, m0 = state0
    state_specs = [
        pl.BlockSpec((1, 1, 2, ML_DVX, ML_DK), lambda b, h: (b, h, 0, 0, 0)),
        pl.BlockSpec((1, 1, 2, 1, LANES), lambda b, h: (b, h, 0, 0, 0)),
    ]
    y, cf, mf = pl.pallas_call(
        functools.partial(_mlstm_kernel, nch=nch),
        grid=(nb, nh),
        in_specs=[
            pl.BlockSpec((tb, ML_DK), lambda b, h: (b, h)),
            pl.BlockSpec((nch, ML_DK, ML_CHUNK), lambda b, h: (b, h, 0)),
            pl.BlockSpec((nch, ML_DV, ML_CHUNK), lambda b, h: (b, h, 0)),
            pl.BlockSpec((tb, ML_DV), lambda b, h: (b, h)),
            pl.BlockSpec((tb, LANES), lambda b, h: (b, h)),
            pl.BlockSpec((nch, 1, 8, ML_CHUNK), lambda b, h: (b, h, 0, 0)),
        ] + state_specs + [pl.BlockSpec((1, ML_DV), lambda b, h: (0, h))],
        out_specs=[pl.BlockSpec((tb, ML_DV), lambda b, h: (b, h))] + state_specs,
        out_shape=[
            jax.ShapeDtypeStruct((st.n_tok, ML_VW), BF16),
            jax.ShapeDtypeStruct((nb, nh, 2, ML_DVX, ML_DK), F32),
            jax.ShapeDtypeStruct((nb, nh, 2, 1, LANES), F32),
        ],
        scratch_shapes=[
            pltpu.VMEM((2, nch, ML_DVX, ML_DK), F32),
            pltpu.VMEM((2, nch, 8, LANES), F32),
        ],
        compiler_params=_cparams(("parallel", "parallel")),
        name="mlstm",
    )(k, qt, vt, so, g, gt, c0, m0, norm_g)
    return y, (cf, mf)


def _residual_ln(x, gate, y, g_ref, b_ref, alpha):
    return _ln(alpha * x + gate * y, g_ref[...], b_ref[...])


def _ev_out_kernel(yml_ref, ysc_ref, x_ref, mod_ref, w_ref, g_ref, b_ref, o_ref, *, alpha):
    for r in range(0, x_ref.shape[0], OUT_ROWS):
        rows = slice(r, r + OUT_ROWS)
        y = _dot(yml_ref[rows, :], w_ref[0, 0:ML_VW, :]) + _dot(ysc_ref[rows, :], w_ref[0, ML_VW:, :])
        o_ref[rows, :] = _residual_ln(x_ref[rows, :], mod_ref[0, 2:3, :], y, g_ref, b_ref, alpha)


def _ev_out(st, yml, ysc, x, mod, layer, w_out, ln_g, ln_b, alpha):
    tm = st.tm
    tok = lambda w: pl.BlockSpec((tm, w), lambda i: (i, 0))
    return pl.pallas_call(
        functools.partial(_ev_out_kernel, alpha=alpha),
        grid=(st.n_tok // tm,),
        in_specs=[tok(ML_VW), tok(SC_W), tok(D_MODEL), st.mod_spec(tm), _resident_layer(w_out, layer),
                  _resident(ln_g.shape), _resident(ln_b.shape)],
        out_specs=tok(D_MODEL),
        out_shape=jax.ShapeDtypeStruct((st.n_tok, D_MODEL), F32),
        compiler_params=_cparams(("parallel",)),
        name="ev_out",
    )(yml, ysc, x, mod, w_out, ln_g, ln_b)


def _ffn_kernel(x_ref, mod_ref, wa_ref, wg_ref, ca_ref, cg_ref, wdn_ref, g_ref, b_ref, o_ref,
                h_scr, acc, *, row_w, alpha):
    j = pl.program_id(1)
    last = pl.num_programs(1) - 1
    tm = h_scr.shape[0]
    rb = max(row_w, FFN_ROWS)
    blocks = [slice(r, r + rb) for r in range(0, tm, rb)]

    def up_gate(first):
        acts = []
        for rows in blocks:
            if first:
                h = _ada(x_ref[rows, :], mod_ref, 3).astype(BF16)
                h_scr[rows, :] = h
            else:
                h = h_scr[rows, :]
            sg = _silu(_conv3(_dot(h, wg_ref[0]), cg_ref[0], row_w))
            a = _conv3(_dot(h, wa_ref[0]), ca_ref[0], row_w)
            acts.append((sg * a).astype(BF16))
        return acts

    @pl.when(j == 0)
    def _():
        acc[...] = _dot(jnp.concatenate(up_gate(True), axis=0), wdn_ref[0])

    @pl.when((j > 0) & (j < last))
    def _():
        acc[...] += _dot(jnp.concatenate(up_gate(False), axis=0), wdn_ref[0])

    @pl.when(j == last)
    def _():
        for rows, act in zip(blocks, up_gate(False)):
            y = acc[rows, :] + _dot(act, wdn_ref[0])
            o_ref[rows, :] = _residual_ln(x_ref[rows, :], mod_ref[0, 5:6, :], y, g_ref, b_ref, alpha)


def _ffn(st, x, mod, layer, w_up, c_up, w_down, ln_g, ln_b, alpha):
    tm, fc = st.tm, FFN_CHUNK
    nf = D_FF // fc
    return pl.pallas_call(
        functools.partial(_ffn_kernel, row_w=st.row_w, alpha=alpha),
        grid=(st.n_tok // tm, nf),
        in_specs=[
            pl.BlockSpec((tm, D_MODEL), lambda i, j: (i, 0)),
            st.mod_spec(tm),
            pl.BlockSpec((1, D_MODEL, fc), lambda i, j: (layer, 0, j)),
            pl.BlockSpec((1, D_MODEL, fc), lambda i, j: (layer, 0, nf + j)),
            pl.BlockSpec((1, 3, fc), lambda i, j: (layer, 0, j)),
            pl.BlockSpec((1, 3, fc), lambda i, j: (layer, 0, nf + j)),
            pl.BlockSpec((1, fc, D_MODEL), lambda i, j: (layer, j, 0)),
            pl.BlockSpec((1, D_MODEL), lambda i, j: (0, 0)),
            pl.BlockSpec((1, D_MODEL), lambda i, j: (0, 0)),
        ],
        out_specs=pl.BlockSpec((tm, D_MODEL), lambda i, j: (i, 0)),
        out_shape=jax.ShapeDtypeStruct((st.n_tok, D_MODEL), F32),
        scratch_shapes=[pltpu.VMEM((tm, D_MODEL), BF16), pltpu.VMEM((tm, D_MODEL), F32)],
        compiler_params=_cparams(("parallel", "arbitrary")),
        name="ffn",
    )(x, mod, w_up, w_up, c_up, c_up, w_down, ln_g, ln_b)


def _od_in_kernel(x_ref, mod_ref, w_ref, sg_ref, sb_ref, ws_ref, bs_ref, ygm_ref, glu_ref):
    gw = GM_W // GM_GROUPS
    hb = _ada(x_ref[...], mod_ref, 0).astype(BF16)
    part = lambda p: w_ref[0, :, p * GM_W:(p + 1) * GM_W]
    glu_ref[...] = (_dot(hb, part(2)) * jax.nn.sigmoid(_dot(hb, part(3)))).astype(BF16)
    vn = _ln(_gelu(_dot(hb, part(1))), sg_ref[...], sb_ref[...]).astype(BF16)
    u = _gelu(_dot(hb, part(0)))
    for ci in range(x_ref.shape[0] // GM_CHUNK):
        rows = slice(ci * GM_CHUNK, (ci + 1) * GM_CHUNK)
        for g in range(GM_GROUPS):
            cols = slice(g * gw, (g + 1) * gw)
            mixed = _dot(ws_ref[g], vn[rows, cols]) + bs_ref[:, g:g + 1]
            ygm_ref[rows, cols] = (u[rows, cols] * mixed).astype(BF16)


def _od_in(st, x, mod, layer, w_in, sgu_g, sgu_b, w_s, b_s_t):
    tm = st.tm_in
    tok = lambda w: pl.BlockSpec((tm, w), lambda i: (i, 0))
    out = jax.ShapeDtypeStruct((st.n_tok, GM_W), BF16)
    consts = (sgu_g, sgu_b, w_s, b_s_t)
    return pl.pallas_call(
        _od_in_kernel,
        grid=(st.n_tok // tm,),
        in_specs=[tok(D_MODEL), st.mod_spec(tm), _resident_layer(w_in, layer)] + [_resident(a.shape) for a in consts],
        out_specs=[tok(GM_W), tok(CF_W)],
        out_shape=[out, out],
        compiler_params=_cparams(("parallel",)),
        name="od_in",
    )(x, mod, w_in, *consts)


def _conv_long_kernel(x_ref, w_ref, o_ref, pad, *, stride, tb):
    half = (CF_CONV // 2) * stride
    cb = x_ref.shape[1]
    pad[0:half, :] = jnp.zeros((half, cb), F32)
    pad[half + tb:, :] = jnp.zeros((half, cb), F32)
    pad[half:half + tb, :] = x_ref[...].astype(F32)
    w = w_ref[...]

    def block(r0, aligned):
        acc = jnp.zeros((CONV_ROWS, cb), F32)
        for j in range(CF_CONV):
            start = r0 + j * stride
            acc = acc + w[j:j + 1, :] * pad[pl.ds(pl.multiple_of(start, 8) if aligned else start, CONV_ROWS), :]
        o_ref[pl.ds(r0, CONV_ROWS), :] = acc.astype(o_ref.dtype)

    n_blocks = tb // CONV_ROWS
    if stride % 8 == 0:
        def body(i, carry):
            block(pl.multiple_of(i * CONV_ROWS, CONV_ROWS), True)
            return carry
        lax.fori_loop(0, n_blocks, body, 0)
    else:
        for i in range(n_blocks):
            block(i * CONV_ROWS, False)


def _conv_long(st, x, w):
    tb, cb = st.t_batch, 256
    stride = st.long_stride
    return pl.pallas_call(
        functools.partial(_conv_long_kernel, stride=stride, tb=tb),
        grid=(st.n_batch, CF_W // cb),
        in_specs=[
            pl.BlockSpec((tb, cb), lambda b, j: (b, j)),
            pl.BlockSpec((CF_CONV, cb), lambda b, j: (0, j)),
        ],
        out_specs=pl.BlockSpec((tb, cb), lambda b, j: (b, j)),
        out_shape=jax.ShapeDtypeStruct((st.n_tok, CF_W), BF16),
        scratch_shapes=[pltpu.VMEM((tb + 2 * (CF_CONV // 2) * stride, cb), F32)],
        compiler_params=_cparams(("parallel", "parallel")),
        name="conv_long",
    )(x, w)


def _od_out_kernel(ygm_ref, cv_ref, x_ref, mod_ref, cg_ref, cb_ref, w_ref, g_ref, b_ref, o_ref, *, alpha):
    for r in range(0, x_ref.shape[0], OUT_ROWS):
        rows = slice(r, r + OUT_ROWS)
        ycf = _silu(_ln(cv_ref[rows, :].astype(F32), cg_ref[...], cb_ref[...])).astype(BF16)
        y = _dot(ygm_ref[rows, :], w_ref[0, 0:GM_W, :]) + _dot(ycf, w_ref[0, GM_W:, :])
        o_ref[rows, :] = _residual_ln(x_ref[rows, :], mod_ref[0, 2:3, :], y, g_ref, b_ref, alpha)


def _od_out(st, ygm, cv, x, mod, layer, cf_g, cf_b, w_out, ln_g, ln_b, alpha):
    tm = st.tm
    tok = lambda w: pl.BlockSpec((tm, w), lambda i: (i, 0))
    return pl.pallas_call(
        functools.partial(_od_out_kernel, alpha=alpha),
        grid=(st.n_tok // tm,),
        in_specs=[tok(GM_W), tok(CF_W), tok(D_MODEL), st.mod_spec(tm), _resident(cf_g.shape), _resident(cf_b.shape),
                  _resident_layer(w_out, layer), _resident(ln_g.shape), _resident(ln_b.shape)],
        out_specs=tok(D_MODEL),
        out_shape=jax.ShapeDtypeStruct((st.n_tok, D_MODEL), F32),
        compiler_params=_cparams(("parallel",)),
        name="od_out",
    )(ygm, cv, x, mod, cf_g, cf_b, w_out, ln_g, ln_b)


def _prep_even(w_in, b_gates, conv_k, conv_q, conv_sc):
    wk = w_in[:, 0:ML_KW]
    wq = w_in[:, EV_Q0:EV_O0]
    wv = w_in[:, EV_V0:EV_G0]
    wo = w_in[:, EV_O0:EV_SC0]
    wsc = jnp.stack([w_in[:, EV_SC0 + i * SC_W:EV_SC0 + (i + 1) * SC_W] for i in range(3)])

    def per_head(g):
        lead = g.shape[:-1]
        g = g.reshape(lead + (2, 2, ML_HEADS))
        g = jnp.moveaxis(g, -1, -3).reshape(lead + (ML_GATES,))
        return jnp.pad(g, [(0, 0)] * len(lead) + [(0, LANES - ML_GATES)])

    wg = per_head(w_in[:, EV_G0:EV_G0 + ML_GATES]).astype(BF16)
    bg = per_head(b_gates.reshape(1, ML_GATES))
    return wk, wq, wv, wo, wsc, wg, bg, conv_k, conv_q, conv_sc


def _zero_state(nb):
    return (jnp.zeros((nb, ML_HEADS, 2, ML_DVX, ML_DK), F32),
            jnp.full((nb, ML_HEADS, 2, 1, LANES), M_INIT, F32))


def kernel(x, c, ctx, c_ctx, w_mod, b_mod, ln1_g, ln1_b, ln2_g, ln2_b, ffn_w_up, ffn_conv, ffn_w_down,
           ev_w_in, ev_b_gates, ev_conv_k, ev_conv_q, ev_norm_g, ev_conv_sc, ev_w_out,
           od_w_in, od_sgu_g, od_sgu_b, od_w_s, od_b_s, od_conv_dw, od_cf_g, od_cf_b, od_w_out):
    nb, seq, d = x.shape
    ctx_len = ctx.shape[1]
    depth = w_mod.shape[0]
    assert d == D_MODEL and seq % TM_LATENT == 0 and ctx_len % ML_CHUNK == 0 and nb + 1 <= 8
    alpha = (2.0 * depth) ** 0.25
    last_even = ((depth - 1) // 2) * 2

    lat = _Stream(nb, seq, TM_LATENT, TM_IN, GRID_W, GRID_W, None)
    cst = _Stream(nb, ctx_len, ctx_len, ctx_len, ctx_len, 1, nb)

    c8 = jnp.zeros((8, d), F32).at[:nb].set(c).at[nb].set(c_ctx)
    mod_all = _modulation(c8, w_mod, b_mod).reshape(depth, 8, 6, d)

    xs = x.reshape(nb * seq, d)
    cs = ctx.reshape(nb * ctx_len, d)
    row = lambda v: v.reshape(1, -1)

    ffn_up_b, ffn_down_b = ffn_w_up.astype(BF16), ffn_w_down.astype(BF16)
    ev_in_b, ev_out_b = ev_w_in.astype(BF16), ev_w_out.astype(BF16)
    od_in_b, od_out_b, od_ws_b = od_w_in.astype(BF16), od_w_out.astype(BF16), od_w_s.astype(BF16)

    for l in range(depth):
        ctx_out = l < last_even
        j = l // 2
        mod = mod_all[l]
        g1, b1, g2, b2 = row(ln1_g[l]), row(ln1_b[l]), row(ln2_g[l]), row(ln2_b[l])

        if l % 2 == 0:
            wts = _prep_even(ev_in_b[j], ev_b_gates[j], ev_conv_k[j], ev_conv_q[j], ev_conv_sc[j])
            norm_g = row(ev_norm_g[j])

            def even_mixer(st, z, state0, want_y):
                k, qt, vt, so, ysc, g, gt = _ev_in(st, z, mod, wts)
                yml, fin = _mlstm(st, k, qt, vt, so, g, gt, state0, norm_g)
                if not want_y:
                    return None, fin
                return _ev_out(st, yml, ysc, z, mod, j, ev_out_b, g1, b1, alpha), fin

            c1, fin = even_mixer(cst, cs, _zero_state(nb), ctx_out)
            x1, _ = even_mixer(lat, xs, fin, True)
        else:
            b_s_t = od_b_s[j].T

            def odd_mixer(st, z):
                ygm, glu = _od_in(st, z, mod, j, od_in_b, row(od_sgu_g[j]), row(od_sgu_b[j]), od_ws_b[j], b_s_t)
                cv = _conv_long(st, glu, od_conv_dw[j])
                return _od_out(st, ygm, cv, z, mod, j, row(od_cf_g[j]), row(od_cf_b[j]), od_out_b, g1, b1, alpha)

            x1 = odd_mixer(lat, xs)
            c1 = odd_mixer(cst, cs) if ctx_out else None

        xs = _ffn(lat, x1, mod, l, ffn_up_b, ffn_conv, ffn_down_b, g2, b2, alpha)
        if ctx_out:
            cs = _ffn(cst, c1, mod, l, ffn_up_b, ffn_conv, ffn_down_b, g2, b2, alpha)

    return xs.reshape(nb, seq, d)
```

```python
import functools

import jax
import jax.numpy as jnp
from jax import lax
from jax.experimental import pallas as pl
from jax.experimental.pallas import tpu as pltpu

F32 = jnp.float32
BF16 = jnp.bfloat16

D_MODEL = 2048
GRID_W = 64
ML_HEADS = 4
ML_DV = 256
ML_DK = 128
ML_VW = ML_HEADS * ML_DV
ML_KW = ML_HEADS * ML_DK
ML_GATES = 16
ML_CHUNK = 128
M_INIT = -1e30
SC_W = 1024
GM_W = 1024
GM_GROUPS = 4
GM_CHUNK = 128
CF_W = 1024
CF_CONV = 31
D_FF = 5632
LN_EPS = 1e-5

EV_V0 = ML_KW
EV_G0 = EV_V0 + ML_VW
EV_Q0 = EV_G0 + ML_GATES
EV_O0 = EV_Q0 + ML_KW
EV_SC0 = EV_O0 + ML_VW

V7X_VMEM_BYTES = 64 * 2**20
VMEM_LIMIT = V7X_VMEM_BYTES - 8 * 2**20
LANES = 128
BF16_ROWS = 16
ML_UNROLL = 8

TM_LATENT = 512
TM_IN = 256
FFN_CHUNK = 512
OUT_ROWS = 256
FFN_ROWS = 256
CONV_ROWS = 64
ML_DVX = ML_DV + BF16_ROWS


def _cparams(sem):
    return pltpu.CompilerParams(dimension_semantics=sem, vmem_limit_bytes=VMEM_LIMIT)


def _silu(z):
    return z * jax.nn.sigmoid(z)


def _gelu(z):
    return 0.5 * z * (1.0 + lax.erf(z * (2.0 ** -0.5)))


def _log_sigmoid(z):
    return jnp.minimum(z, 0.0) - jnp.log1p(jnp.exp(-jnp.abs(z)))


def _ln(z, g, b):
    mu = jnp.mean(z, axis=-1, keepdims=True)
    zc = z - mu
    var = jnp.mean(zc * zc, axis=-1, keepdims=True)
    return zc * lax.rsqrt(var + LN_EPS) * g + b


def _ada(x, mod_ref, slot):
    return x * (1.0 + mod_ref[0, slot + 1:slot + 2, :]) + mod_ref[0, slot:slot + 1, :]


def _conv3(a, w, row_w):
    tm = a.shape[0]
    t = lax.broadcasted_iota(jnp.int32, (tm, 1), 0) % row_w
    prev = jnp.where(t == 0, 0.0, pltpu.roll(a, 1, 0))
    nxt = jnp.where(t == row_w - 1, 0.0, pltpu.roll(a, tm - 1, 0))
    return w[0:1, :] * prev + w[1:2, :] * a + w[2:3, :] * nxt


def _dot(a, b):
    return jnp.dot(a, b, preferred_element_type=F32)


def _tri_sum(tri, z):
    hi = z.astype(BF16)
    r1 = z - hi.astype(F32)
    mid = r1.astype(BF16)
    lo = (r1 - mid.astype(F32)).astype(BF16)
    return _dot(tri, hi) + _dot(tri, mid) + _dot(tri, lo)


def _resident(shape):
    zeros = (0,) * len(shape)
    return pl.BlockSpec(shape, lambda *_: zeros, pipeline_mode=pl.Buffered(1))


def _resident_layer(stacked, layer):
    tail = (0,) * (stacked.ndim - 1)
    return pl.BlockSpec((1,) + stacked.shape[1:], lambda *_: (layer,) + tail, pipeline_mode=pl.Buffered(1))


def _mod_kernel(c_ref, w_ref, b_ref, o_ref):
    s = _silu(c_ref[...]).astype(BF16)
    o_ref[0] = _dot(s, w_ref[0].astype(BF16)) + b_ref[0]


def _modulation(c8, w_mod, b_mod):
    depth, d, n = w_mod.shape
    tn = 1024
    return pl.pallas_call(
        _mod_kernel,
        grid=(depth, n // tn),
        in_specs=[
            pl.BlockSpec((8, d), lambda l, j: (0, 0)),
            pl.BlockSpec((1, d, tn), lambda l, j: (l, 0, j)),
            pl.BlockSpec((1, 1, tn), lambda l, j: (l, 0, j)),
        ],
        out_specs=pl.BlockSpec((1, 8, tn), lambda l, j: (l, 0, j)),
        out_shape=jax.ShapeDtypeStruct((depth, 8, n), F32),
        compiler_params=_cparams(("parallel", "parallel")),
        name="modulation",
    )(c8, w_mod, b_mod.reshape(depth, 1, n))


class _Stream:
    def __init__(self, n_batch, t_batch, tm, tm_in, row_w, long_stride, mod_row):
        self.n_batch = n_batch
        self.t_batch = t_batch
        self.tm = tm
        self.tm_in = tm_in
        self.row_w = row_w
        self.long_stride = long_stride
        self.mod_row = mod_row
        self.n_tok = n_batch * t_batch

    def mod_spec(self, tm):
        if self.mod_row is None:
            tpb = self.t_batch // tm
            return pl.BlockSpec((1, 6, D_MODEL), lambda i, *_: (i // tpb, 0, 0))
        r = self.mod_row
        return pl.BlockSpec((1, 6, D_MODEL), lambda i, *_: (r, 0, 0))


def _ev_in_kernel(x_ref, mod_ref, wk_ref, wq_ref, wv_ref, wo_ref, wsc_ref, wg_ref, bg_ref, ck_ref, cq_ref, csc_ref,
                  k_ref, qt_ref, vt_ref, so_ref, ysc_ref, g_ref, gt_ref, *, row_w):
    L = ML_CHUNK
    tm = x_ref.shape[0]
    chunks = [slice(ci * L, (ci + 1) * L) for ci in range(tm // L)]
    hb = _ada(x_ref[...], mod_ref, 0).astype(BF16)

    k = _silu(_conv3(_dot(hb, wk_ref[...]), ck_ref[...], row_w)) * (ML_DK ** -0.5)
    k_ref[...] = k.astype(BF16)

    q = _silu(_conv3(_dot(hb, wq_ref[...]), cq_ref[...], row_w))
    for ci, rows in enumerate(chunks):
        for f in range(ML_KW // L):
            qt_ref[ci, f * L:(f + 1) * L, :] = q[rows, f * L:(f + 1) * L].T.astype(BF16)

    v = _dot(hb, wv_ref[...])
    for ci, rows in enumerate(chunks):
        for f in range(ML_VW // L):
            vt_ref[ci, f * L:(f + 1) * L, :] = v[rows, f * L:(f + 1) * L].T.astype(BF16)

    so_ref[...] = jax.nn.sigmoid(_dot(hb, wo_ref[...])).astype(BF16)

    gate_c = _dot(hb, wsc_ref[0])
    val = _dot(hb, wsc_ref[2])
    mixed = _conv3(gate_c * val, csc_ref[...], row_w)
    ysc_ref[...] = (_dot(hb, wsc_ref[1]) * mixed).astype(BF16)

    g = _dot(hb, wg_ref[...]) + bg_ref[...]
    kind = lax.broadcasted_iota(jnp.int32, (L, LANES), 1) % 4
    r_i = lax.broadcasted_iota(jnp.int32, (L, L), 0)
    c_i = lax.broadcasted_iota(jnp.int32, (L, L), 1)
    lower = (c_i <= r_i).astype(BF16)
    upper = (c_i >= r_i).astype(BF16)
    for ci, rows in enumerate(chunks):
        gc = g[rows, :]
        lf = _log_sigmoid(gc)
        out = jnp.where(kind == 1, _tri_sum(lower, lf), jnp.where(kind == 3, _tri_sum(upper, lf), gc))
        for hd in range(ML_HEADS):
            mine = out if hd == 0 else pltpu.roll(out, LANES - 4 * hd, 1)
            g_ref[rows, hd * LANES:(hd + 1) * LANES] = mine
            gt_ref[ci, hd] = mine.T[0:8, :]


def _ev_in(st, x, mod, wts):
    wk, wq, wv, wo, wsc, wg, bg, ck, cq, csc = wts
    tm = st.tm_in
    nck = tm // ML_CHUNK
    n_chunks = st.n_tok // ML_CHUNK
    tok = lambda w: pl.BlockSpec((tm, w), lambda i: (i, 0))
    return pl.pallas_call(
        functools.partial(_ev_in_kernel, row_w=st.row_w),
        grid=(st.n_tok // tm,),
        in_specs=[tok(D_MODEL), st.mod_spec(tm)] + [_resident(w.shape) for w in wts],
        out_specs=[
            tok(ML_KW),
            pl.BlockSpec((nck, ML_KW, ML_CHUNK), lambda i: (i, 0, 0)),
            pl.BlockSpec((nck, ML_VW, ML_CHUNK), lambda i: (i, 0, 0)),
            tok(ML_VW),
            tok(SC_W),
            tok(ML_HEADS * LANES),
            pl.BlockSpec((nck, ML_HEADS, 8, ML_CHUNK), lambda i: (i, 0, 0, 0)),
        ],
        out_shape=[
            jax.ShapeDtypeStruct((st.n_tok, ML_KW), BF16),
            jax.ShapeDtypeStruct((n_chunks, ML_KW, ML_CHUNK), BF16),
            jax.ShapeDtypeStruct((n_chunks, ML_VW, ML_CHUNK), BF16),
            jax.ShapeDtypeStruct((st.n_tok, ML_VW), BF16),
            jax.ShapeDtypeStruct((st.n_tok, SC_W), BF16),
            jax.ShapeDtypeStruct((st.n_tok, ML_HEADS * LANES), F32),
            jax.ShapeDtypeStruct((n_chunks, ML_HEADS, 8, ML_CHUNK), F32),
        ],
        compiler_params=_cparams(("parallel",)),
        name="ev_in",
    )(x, mod, *wts)


def _mlstm_kernel(k_ref, qt_ref, vt_ref, so_ref, g_ref, gt_ref, c0_ref, m0_ref, ng_ref,
                  y_ref, cf_ref, mf_ref, st_scr, nm_scr, *, nch):
    L = ML_CHUNK
    s_i = lax.broadcasted_iota(jnp.int32, (L, L), 0)
    t_i = lax.broadcasted_iota(jnp.int32, (L, L), 1)
    row_8 = lax.broadcasted_iota(jnp.int32, (8, LANES), 0)

    def gate_rows(g8, d):
        ig_row, b_row = g8[2 * d:2 * d + 1, :], g8[2 * d + 1:2 * d + 2, :]
        b_last = b_row[:, L - 1:L] if d == 0 else b_row[:, 0:1]
        return ig_row, b_row, b_last

    def local_pass(c, carry):
        r = pl.multiple_of(c * L, L)
        g8 = gt_ref[c, 0]
        kc = k_ref[pl.ds(r, L), :]
        vtc = vt_ref[c].astype(F32)
        for d in range(2):
            ig_row, b_row, b_last = gate_rows(g8, d)
            a_row = b_last - b_row + ig_row
            m_loc = jnp.max(a_row, axis=-1, keepdims=True)
            w_row = jnp.exp(a_row - m_loc)
            lhs = jnp.concatenate([vtc * w_row, jnp.broadcast_to(w_row, (BF16_ROWS, L))], axis=0)
            st_scr[d, c] = _dot(lhs.astype(BF16), kc)
            nm_scr[d, c] = jnp.where(row_8 == 0, m_loc, b_last)
        return carry

    lax.fori_loop(0, nch, local_pass, 0, unroll=min(ML_UNROLL, nch))

    for d in range(2):
        def stabiliser_scan(it, m_prev, d=d):
            c = it if d == 0 else nch - 1 - it
            blk = nm_scr[d, c]
            m_loc, b_last = blk[0:1, :], blk[1:2, :]
            m_new = jnp.maximum(b_last + m_prev, m_loc)
            e_prev = jnp.exp(b_last + m_prev - m_new)
            e_loc = jnp.exp(m_loc - m_new)
            nm_scr[d, c] = jnp.where(row_8 == 0, m_prev, jnp.where(row_8 == 1, e_prev, e_loc))
            return m_new

        def state_scan(it, cx, d=d):
            c = it if d == 0 else nch - 1 - it
            loc = st_scr[d, c]
            blk = nm_scr[d, c]
            st_scr[d, c] = cx
            return blk[1:2, :] * cx + blk[2:3, :] * loc

        mf_ref[0, 0, d] = lax.fori_loop(0, nch, stabiliser_scan, m0_ref[0, 0, d])
        cf_ref[0, 0, d] = lax.fori_loop(0, nch, state_scan, c0_ref[0, 0, d])

    def output_pass(c, carry):
        r = pl.multiple_of(c * L, L)
        g8 = gt_ref[c, 0]
        gch = g_ref[pl.ds(r, L), :]
        qtc = qt_ref[c]
        vtc = vt_ref[c]
        sraw = _dot(k_ref[pl.ds(r, L), :], qtc)
        h = None
        for d in range(2):
            ig_row, b_row, _ = gate_rows(g8, d)
            src_col = gch[:, 2 * d:2 * d + 1] - gch[:, 2 * d + 1:2 * d + 2]
            visible = (s_i <= t_i) if d == 0 else (s_i >= t_i)
            log_intra = jnp.where(visible, src_col + b_row, -jnp.inf)
            m_prev = nm_scr[d, c][0:1, 0:1]
            log_inter = b_row + m_prev
            m_t = jnp.maximum(log_inter, jnp.max(log_intra, axis=0, keepdims=True))
            s = sraw * jnp.exp(log_intra - m_t)
            w_inter = jnp.exp(log_inter - m_t)
            inter = _dot(st_scr[d, c].astype(BF16), qtc)
            num = _dot(vtc, s.astype(BF16)) + w_inter * inter[0:ML_DV, :]
            den = jnp.sum(s, axis=0, keepdims=True) + w_inter * inter[ML_DV:ML_DV + 1, :]
            hd = num * (1.0 / jnp.maximum(jnp.abs(den), jnp.exp(-m_t)))
            h = hd if h is None else h + hd
        mu = jnp.mean(h, axis=0, keepdims=True)
        hc = h - mu
        var = jnp.mean(hc * hc, axis=0, keepdims=True)
        hn = (hc * lax.rsqrt(var + LN_EPS)).T
        y_ref[pl.ds(r, L), :] = (hn * ng_ref[...] * so_ref[pl.ds(r, L), :].astype(F32)).astype(BF16)
        return carry

    lax.fori_loop(0, nch, output_pass, 0, unroll=min(ML_UNROLL, nch))


def _mlstm(st, k, qt, vt, so, g, gt, state0, norm_g):
    tb = st.t_batch
    nch = tb // ML_CHUNK
    nb, nh = st.n_batch, ML_HEADS
    c0, m0 = state0
    state_specs = [
        pl.BlockSpec((1, 1, 2, ML_DVX, ML_DK), lambda b, h: (b, h, 0, 0, 0)),
        pl.BlockSpec((1, 1, 2, 1, LANES), lambda b, h: (b, h, 0, 0, 0)),
    ]
    y, cf, mf = pl.pallas_call(
        functools.partial(_mlstm_kernel, nch=nch),
        grid=(nb, nh),
        in_specs=[
            pl.BlockSpec((tb, ML_DK), lambda b, h: (b, h)),
            pl.BlockSpec((nch, ML_DK, ML_CHUNK), lambda b, h: (b, h, 0)),
            pl.BlockSpec((nch, ML_DV, ML_CHUNK), lambda b, h: (b, h, 0)),
            pl.BlockSpec((tb, ML_DV), lambda b, h: (b, h)),
            pl.BlockSpec((tb, LANES), lambda b, h: (b, h)),
            pl.BlockSpec((nch, 1, 8, ML_CHUNK), lambda b, h: (b, h, 0, 0)),
        ] + state_specs + [pl.BlockSpec((1, ML_DV), lambda b, h: (0, h))],
        out_specs=[pl.BlockSpec((tb, ML_DV), lambda b, h: (b, h))] + state_specs,
        out_shape=[
            jax.ShapeDtypeStruct((st.n_tok, ML_VW), BF16),
            jax.ShapeDtypeStruct((nb, nh, 2, ML_DVX, ML_DK), F32),
            jax.ShapeDtypeStruct((nb, nh, 2, 1, LANES), F32),
        ],
        scratch_shapes=[
            pltpu.VMEM((2, nch, ML_DVX, ML_DK), F32),
            pltpu.VMEM((2, nch, 8, LANES), F32),
        ],
        compiler_params=_cparams(("parallel", "parallel")),
        name="mlstm",
    )(k, qt, vt, so, g, gt, c0, m0, norm_g)
    return y, (cf, mf)


def _residual_ln(x, gate, y, g_ref, b_ref, alpha):
    return _ln(alpha * x + gate * y, g_ref[...], b_ref[...])


def _ev_out_kernel(yml_ref, ysc_ref, x_ref, mod_ref, w_ref, g_ref, b_ref, o_ref, *, alpha):
    for r in range(0, x_ref.shape[0], OUT_ROWS):
        rows = slice(r, r + OUT_ROWS)
        y = _dot(yml_ref[rows, :], w_ref[0, 0:ML_VW, :]) + _dot(ysc_ref[rows, :], w_ref[0, ML_VW:, :])
        o_ref[rows, :] = _residual_ln(x_ref[rows, :], mod_ref[0, 2:3, :], y, g_ref, b_ref, alpha)


def _ev_out(st, yml, ysc, x, mod, layer, w_out, ln_g, ln_b, alpha):
    tm = st.tm
    tok = lambda w: pl.BlockSpec((tm, w), lambda i: (i, 0))
    return pl.pallas_call(
        functools.partial(_ev_out_kernel, alpha=alpha),
        grid=(st.n_tok // tm,),
        in_specs=[tok(ML_VW), tok(SC_W), tok(D_MODEL), st.mod_spec(tm), _resident_layer(w_out, layer),
                  _resident(ln_g.shape), _resident(ln_b.shape)],
        out_specs=tok(D_MODEL),
        out_shape=jax.ShapeDtypeStruct((st.n_tok, D_MODEL), F32),
        compiler_params=_cparams(("parallel",)),
        name="ev_out",
    )(yml, ysc, x, mod, w_out, ln_g, ln_b)


def _ffn_kernel(x_ref, mod_ref, wa_ref, wg_ref, ca_ref, cg_ref, wdn_ref, g_ref, b_ref, o_ref,
                h_scr, acc, *, row_w, alpha):
    j = pl.program_id(1)
    last = pl.num_programs(1) - 1
    tm = h_scr.shape[0]
    rb = max(row_w, FFN_ROWS)
    blocks = [slice(r, r + rb) for r in range(0, tm, rb)]

    def up_gate(first):
        acts = []
        for rows in blocks:
            if first:
                h = _ada(x_ref[rows, :], mod_ref, 3).astype(BF16)
                h_scr[rows, :] = h
            else:
                h = h_scr[rows, :]
            sg = _silu(_conv3(_dot(h, wg_ref[0]), cg_ref[0], row_w))
            a = _conv3(_dot(h, wa_ref[0]), ca_ref[0], row_w)
            acts.append((sg * a).astype(BF16))
        return acts

    @pl.when(j == 0)
    def _():
        acc[...] = _dot(jnp.concatenate(up_gate(True), axis=0), wdn_ref[0])

    @pl.when((j > 0) & (j < last))
    def _():
        acc[...] += _dot(jnp.concatenate(up_gate(False), axis=0), wdn_ref[0])

    @pl.when(j == last)
    def _():
        for rows, act in zip(blocks, up_gate(False)):
            y = acc[rows, :] + _dot(act, wdn_ref[0])
            o_ref[rows, :] = _residual_ln(x_ref[rows, :], mod_ref[0, 5:6, :], y, g_ref, b_ref, alpha)


def _ffn(st, x, mod, layer, w_up, c_up, w_down, ln_g, ln_b, alpha):
    tm, fc = st.tm, FFN_CHUNK
    nf = D_FF // fc
    return pl.pallas_call(
        functools.partial(_ffn_kernel, row_w=st.row_w, alpha=alpha),
        grid=(st.n_tok // tm, nf),
        in_specs=[
            pl.BlockSpec((tm, D_MODEL), lambda i, j: (i, 0)),
            st.mod_spec(tm),
            pl.BlockSpec((1, D_MODEL, fc), lambda i, j: (layer, 0, j)),
            pl.BlockSpec((1, D_MODEL, fc), lambda i, j: (layer, 0, nf + j)),
            pl.BlockSpec((1, 3, fc), lambda i, j: (layer, 0, j)),
            pl.BlockSpec((1, 3, fc), lambda i, j: (layer, 0, nf + j)),
            pl.BlockSpec((1, fc, D_MODEL), lambda i, j: (layer, j, 0)),
            pl.BlockSpec((1, D_MODEL), lambda i, j: (0, 0)),
            pl.BlockSpec((1, D_MODEL), lambda i, j: (0, 0)),
        ],
        out_specs=pl.BlockSpec((tm, D_MODEL), lambda i, j: (i, 0)),
        out_shape=jax.ShapeDtypeStruct((st.n_tok, D_MODEL), F32),
        scratch_shapes=[pltpu.VMEM((tm, D_MODEL), BF16), pltpu.VMEM((tm, D_MODEL), F32)],
        compiler_params=_cparams(("parallel", "arbitrary")),
        name="ffn",
    )(x, mod, w_up, w_up, c_up, c_up, w_down, ln_g, ln_b)


def _od_in_kernel(x_ref, mod_ref, w_ref, sg_ref, sb_ref, ws_ref, bs_ref, ygm_ref, glu_ref):
    gw = GM_W // GM_GROUPS
    hb = _ada(x_ref[...], mod_ref, 0).astype(BF16)
    part = lambda p: w_ref[0, :, p * GM_W:(p + 1) * GM_W]
    glu_ref[...] = (_dot(hb, part(2)) * jax.nn.sigmoid(_dot(hb, part(3)))).astype(BF16)
    vn = _ln(_gelu(_dot(hb, part(1))), sg_ref[...], sb_ref[...]).astype(BF16)
    u = _gelu(_dot(hb, part(0)))
    for ci in range(x_ref.shape[0] // GM_CHUNK):
        rows = slice(ci * GM_CHUNK, (ci + 1) * GM_CHUNK)
        for g in range(GM_GROUPS):
            cols = slice(g * gw, (g + 1) * gw)
            mixed = _dot(ws_ref[g], vn[rows, cols]) + bs_ref[:, g:g + 1]
            ygm_ref[rows, cols] = (u[rows, cols] * mixed).astype(BF16)


def _od_in(st, x, mod, layer, w_in, sgu_g, sgu_b, w_s, b_s_t):
    tm = st.tm
    tok = lambda w: pl.BlockSpec((tm, w), lambda i: (i, 0))
    out = jax.ShapeDtypeStruct((st.n_tok, GM_W), BF16)
    consts = (sgu_g, sgu_b, w_s, b_s_t)
    return pl.pallas_call(
        _od_in_kernel,
        grid=(st.n_tok // tm,),
        in_specs=[tok(D_MODEL), st.mod_spec(tm), _resident_layer(w_in, layer)] + [_resident(a.shape) for a in consts],
        out_specs=[tok(GM_W), tok(CF_W)],
        out_shape=[out, out],
        compiler_params=_cparams(("parallel",)),
        name="od_in",
    )(x, mod, w_in, *consts)


def _conv_long_kernel(x_ref, w_ref, o_ref, pad, *, stride, tb):
    half = (CF_CONV // 2) * stride
    cb = x_ref.shape[1]
    pad[0:half, :] = jnp.zeros((half, cb), F32)
    pad[half + tb:, :] = jnp.zeros((half, cb), F32)
    pad[half:half + tb, :] = x_ref[...].astype(F32)
    w = w_ref[...]

    def block(r0, aligned):
        acc = jnp.zeros((CONV_ROWS, cb), F32)
        for j in range(CF_CONV):
            start = r0 + j * stride
            acc = acc + w[j:j + 1, :] * pad[pl.ds(pl.multiple_of(start, 8) if aligned else start, CONV_ROWS), :]
        o_ref[pl.ds(r0, CONV_ROWS), :] = acc.astype(o_ref.dtype)

    n_blocks = tb // CONV_ROWS
    if stride % 8 == 0:
        def body(i, carry):
            block(pl.multiple_of(i * CONV_ROWS, CONV_ROWS), True)
            return carry
        lax.fori_loop(0, n_blocks, body, 0)
    else:
        for i in range(n_blocks):
            block(i * CONV_ROWS, False)


def _conv_long(st, x, w):
    tb, cb = st.t_batch, 256
    stride = st.long_stride
    return pl.pallas_call(
        functools.partial(_conv_long_kernel, stride=stride, tb=tb),
        grid=(st.n_batch, CF_W // cb),
        in_specs=[
            pl.BlockSpec((tb, cb), lambda b, j: (b, j)),
            pl.BlockSpec((CF_CONV, cb), lambda b, j: (0, j)),
        ],
        out_specs=pl.BlockSpec((tb, cb), lambda b, j: (b, j)),
        out_shape=jax.ShapeDtypeStruct((st.n_tok, CF_W), BF16),
        scratch_shapes=[pltpu.VMEM((tb + 2 * (CF_CONV // 2) * stride, cb), F32)],
        compiler_params=_cparams(("parallel", "parallel")),
        name="conv_long",
    )(x, w)


def _od_out_kernel(ygm_ref, cv_ref, x_ref, mod_ref, cg_ref, cb_ref, w_ref, g_ref, b_ref, o_ref, *, alpha):
    for r in range(0, x_ref.shape[0], OUT_ROWS):
        rows = slice(r, r + OUT_ROWS)
        ycf = _silu(_ln(cv_ref[rows, :].astype(F32), cg_ref[...], cb_ref[...])).astype(BF16)
        y = _dot(ygm_ref[rows, :], w_ref[0, 0:GM_W, :]) + _dot(ycf, w_ref[0, GM_W:, :])
        o_ref[rows, :] = _residual_ln(x_ref[rows, :], mod_ref[0, 2:3, :], y, g_ref, b_ref, alpha)


def _od_out(st, ygm, cv, x, mod, layer, cf_g, cf_b, w_out, ln_g, ln_b, alpha):
    tm = st.tm
    tok = lambda w: pl.BlockSpec((tm, w), lambda i: (i, 0))
    return pl.pallas_call(
        functools.partial(_od_out_kernel, alpha=alpha),
        grid=(st.n_tok // tm,),
        in_specs=[tok(GM_W), tok(CF_W), tok(D_MODEL), st.mod_spec(tm), _resident(cf_g.shape), _resident(cf_b.shape),
                  _resident_layer(w_out, layer), _resident(ln_g.shape), _resident(ln_b.shape)],
        out_specs=tok(D_MODEL),
        out_shape=jax.ShapeDtypeStruct((st.n_tok, D_MODEL), F32),
        compiler_params=_cparams(("parallel",)),
        name="od_out",
    )(ygm, cv, x, mod, cf_g, cf_b, w_out, ln_g, ln_b)


def _prep_even(w_in, b_gates, conv_k, conv_q, conv_sc):
    wk = w_in[:, 0:ML_KW]
    wq = w_in[:, EV_Q0:EV_O0]
    wv = w_in[:, EV_V0:EV_G0]
    wo = w_in[:, EV_O0:EV_SC0]
    wsc = jnp.stack([w_in[:, EV_SC0 + i * SC_W:EV_SC0 + (i + 1) * SC_W] for i in range(3)])

    def per_head(g):
        lead = g.shape[:-1]
        g = g.reshape(lead + (2, 2, ML_HEADS))
        g = jnp.moveaxis(g, -1, -3).reshape(lead + (ML_GATES,))
        return jnp.pad(g, [(0, 0)] * len(lead) + [(0, LANES - ML_GATES)])

    wg = per_head(w_in[:, EV_G0:EV_G0 + ML_GATES]).astype(BF16)
    bg = per_head(b_gates.reshape(1, ML_GATES))
    return wk, wq, wv, wo, wsc, wg, bg, conv_k, conv_q, conv_sc


def _zero_state(nb):
    return (jnp.zeros((nb, ML_HEADS, 2, ML_DVX, ML_DK), F32),
            jnp.full((nb, ML_HEADS, 2, 1, LANES), M_INIT, F32))


def kernel(x, c, ctx, c_ctx, w_mod, b_mod, ln1_g, ln1_b, ln2_g, ln2_b, ffn_w_up, ffn_conv, ffn_w_down,
           ev_w_in, ev_b_gates, ev_conv_k, ev_conv_q, ev_norm_g, ev_conv_sc, ev_w_out,
           od_w_in, od_sgu_g, od_sgu_b, od_w_s, od_b_s, od_conv_dw, od_cf_g, od_cf_b, od_w_out):
    nb, seq, d = x.shape
    ctx_len = ctx.shape[1]
    depth = w_mod.shape[0]
    assert d == D_MODEL and seq % TM_LATENT == 0 and ctx_len % ML_CHUNK == 0 and nb + 1 <= 8
    alpha = (2.0 * depth) ** 0.25
    last_even = ((depth - 1) // 2) * 2

    lat = _Stream(nb, seq, TM_LATENT, TM_IN, GRID_W, GRID_W, None)
    ctx_tm = ctx_len * max(k for k in range(1, nb + 1) if nb % k == 0 and k * ctx_len <= TM_LATENT)
    cst = _Stream(nb, ctx_len, ctx_tm, ctx_len, ctx_len, 1, nb)

    c8 = jnp.zeros((8, d), F32).at[:nb].set(c).at[nb].set(c_ctx)
    mod_all = _modulation(c8, w_mod, b_mod).reshape(depth, 8, 6, d)

    xs = x.reshape(nb * seq, d)
    cs = ctx.reshape(nb * ctx_len, d)
    row = lambda v: v.reshape(1, -1)

    ffn_up_b, ffn_down_b = ffn_w_up.astype(BF16), ffn_w_down.astype(BF16)
    ev_in_b, ev_out_b = ev_w_in.astype(BF16), ev_w_out.astype(BF16)
    od_in_b, od_out_b, od_ws_b = od_w_in.astype(BF16), od_w_out.astype(BF16), od_w_s.astype(BF16)

    for l in range(depth):
        ctx_out = l < last_even
        j = l // 2
        mod = mod_all[l]
        g1, b1, g2, b2 = row(ln1_g[l]), row(ln1_b[l]), row(ln2_g[l]), row(ln2_b[l])

        if l % 2 == 0:
            wts = _prep_even(ev_in_b[j], ev_b_gates[j], ev_conv_k[j], ev_conv_q[j], ev_conv_sc[j])
            norm_g = row(ev_norm_g[j])

            def even_mixer(st, z, state0, want_y):
                k, qt, vt, so, ysc, g, gt = _ev_in(st, z, mod, wts)
                yml, fin = _mlstm(st, k, qt, vt, so, g, gt, state0, norm_g)
                if not want_y:
                    return None, fin
                return _ev_out(st, yml, ysc, z, mod, j, ev_out_b, g1, b1, alpha), fin

            c1, fin = even_mixer(cst, cs, _zero_state(nb), ctx_out)
            x1, _ = even_mixer(lat, xs, fin, True)
        else:
            b_s_t = od_b_s[j].T

            def odd_mixer(st, z):
                ygm, glu = _od_in(st, z, mod, j, od_in_b, row(od_sgu_g[j]), row(od_sgu_b[j]), od_ws_b[j], b_s_t)
                cv = _conv_long(st, glu, od_conv_dw[j])
                return _od_out(st, ygm, cv, z, mod, j, row(od_cf_g[j]), row(od_cf_b[j]), od_out_b, g1, b1, alpha)

            x1 = odd_mixer(lat, xs)
            c1 = odd_mixer(cst, cs) if ctx_out else None

        xs = _ffn(lat, x1, mod, l, ffn_up_b, ffn_conv, ffn_down_b, g2, b2, alpha)
        if ctx_out:
            cs = _ffn(cst, c1, mod, l, ffn_up_b, ffn_conv, ffn_down_b, g2, b2, alpha)

    return xs.reshape(nb, seq, d)
```

```python
import functools

import jax
import jax.numpy as jnp
from jax import lax
from jax.experimental import pallas as pl
from jax.experimental.pallas import tpu as pltpu

F32 = jnp.float32
BF16 = jnp.bfloat16

D_MODEL = 2048
GRID_W = 64
ML_HEADS = 4
ML_DV = 256
ML_DK = 128
ML_VW = ML_HEADS * ML_DV
ML_KW = ML_HEADS * ML_DK
ML_GATES = 16
ML_CHUNK = 128
M_INIT = -1e30
SC_W = 1024
GM_W = 1024
GM_GROUPS = 4
GM_CHUNK = 128
CF_W = 1024
CF_CONV = 31
D_FF = 5632
LN_EPS = 1e-5

EV_V0 = ML_KW
EV_G0 = EV_V0 + ML_VW
EV_Q0 = EV_G0 + ML_GATES
EV_O0 = EV_Q0 + ML_KW
EV_SC0 = EV_O0 + ML_VW

V7X_VMEM_BYTES = 64 * 2**20
VMEM_LIMIT = V7X_VMEM_BYTES - 8 * 2**20
LANES = 128
BF16_ROWS = 16
ML_UNROLL = 8

TM_LATENT = 512
TM_IN = 256
FFN_CHUNK = 512
OUT_ROWS = 256
FFN_ROWS = 512
CONV_ROWS = 64
ML_DVX = ML_DV + BF16_ROWS


def _cparams(sem):
    return pltpu.CompilerParams(dimension_semantics=sem, vmem_limit_bytes=VMEM_LIMIT)


def _silu(z):
    return z * jax.nn.sigmoid(z)


def _gelu(z):
    return 0.5 * z * (1.0 + lax.erf(z * (2.0 ** -0.5)))


def _log_sigmoid(z):
    return jnp.minimum(z, 0.0) - jnp.log1p(jnp.exp(-jnp.abs(z)))


def _ln(z, g, b):
    mu = jnp.mean(z, axis=-1, keepdims=True)
    zc = z - mu
    var = jnp.mean(zc * zc, axis=-1, keepdims=True)
    return zc * lax.rsqrt(var + LN_EPS) * g + b


def _ada(x, mod_ref, slot):
    return x * (1.0 + mod_ref[0, slot + 1:slot + 2, :]) + mod_ref[0, slot:slot + 1, :]


def _conv3(a, w, row_w):
    tm = a.shape[0]
    t = lax.broadcasted_iota(jnp.int32, (tm, 1), 0) % row_w
    prev = jnp.where(t == 0, 0.0, pltpu.roll(a, 1, 0))
    nxt = jnp.where(t == row_w - 1, 0.0, pltpu.roll(a, tm - 1, 0))
    return w[0:1, :] * prev + w[1:2, :] * a + w[2:3, :] * nxt


def _dot(a, b):
    return jnp.dot(a, b, preferred_element_type=F32)


def _tri_sum(tri, z):
    hi = z.astype(BF16)
    r1 = z - hi.astype(F32)
    mid = r1.astype(BF16)
    lo = (r1 - mid.astype(F32)).astype(BF16)
    return _dot(tri, hi) + _dot(tri, mid) + _dot(tri, lo)


def _resident(shape):
    zeros = (0,) * len(shape)
    return pl.BlockSpec(shape, lambda *_: zeros, pipeline_mode=pl.Buffered(1))


def _resident_layer(stacked, layer):
    tail = (0,) * (stacked.ndim - 1)
    return pl.BlockSpec((1,) + stacked.shape[1:], lambda *_: (layer,) + tail, pipeline_mode=pl.Buffered(1))


def _mod_kernel(c_ref, w_ref, b_ref, o_ref):
    s = _silu(c_ref[...]).astype(BF16)
    o_ref[0] = _dot(s, w_ref[0].astype(BF16)) + b_ref[0]


def _modulation(c8, w_mod, b_mod):
    depth, d, n = w_mod.shape
    tn = 1024
    return pl.pallas_call(
        _mod_kernel,
        grid=(depth, n // tn),
        in_specs=[
            pl.BlockSpec((8, d), lambda l, j: (0, 0)),
            pl.BlockSpec((1, d, tn), lambda l, j: (l, 0, j)),
            pl.BlockSpec((1, 1, tn), lambda l, j: (l, 0, j)),
        ],
        out_specs=pl.BlockSpec((1, 8, tn), lambda l, j: (l, 0, j)),
        out_shape=jax.ShapeDtypeStruct((depth, 8, n), F32),
        compiler_params=_cparams(("parallel", "parallel")),
        name="modulation",
    )(c8, w_mod, b_mod.reshape(depth, 1, n))


class _Stream:
    def __init__(self, n_batch, t_batch, tm, tm_in, row_w, long_stride, mod_row):
        self.n_batch = n_batch
        self.t_batch = t_batch
        self.tm = tm
        self.tm_in = tm_in
        self.row_w = row_w
        self.long_stride = long_stride
        self.mod_row = mod_row
        self.n_tok = n_batch * t_batch

    def mod_spec(self, tm):
        if self.mod_row is None:
            tpb = self.t_batch // tm
            return pl.BlockSpec((1, 6, D_MODEL), lambda i, *_: (i // tpb, 0, 0))
        r = self.mod_row
        return pl.BlockSpec((1, 6, D_MODEL), lambda i, *_: (r, 0, 0))


def _ev_in_kernel(x_ref, mod_ref, wk_ref, wq_ref, wv_ref, wo_ref, wsc_ref, wg_ref, bg_ref, ck_ref, cq_ref, csc_ref,
                  k_ref, qt_ref, vt_ref, so_ref, ysc_ref, g_ref, gt_ref, *, row_w):
    L = ML_CHUNK
    tm = x_ref.shape[0]
    chunks = [slice(ci * L, (ci + 1) * L) for ci in range(tm // L)]
    hb = _ada(x_ref[...], mod_ref, 0).astype(BF16)

    k = _silu(_conv3(_dot(hb, wk_ref[...]), ck_ref[...], row_w)) * (ML_DK ** -0.5)
    k_ref[...] = k.astype(BF16)

    q = _silu(_conv3(_dot(hb, wq_ref[...]), cq_ref[...], row_w))
    for ci, rows in enumerate(chunks):
        for f in range(ML_KW // L):
            qt_ref[ci, f * L:(f + 1) * L, :] = q[rows, f * L:(f + 1) * L].T.astype(BF16)

    v = _dot(hb, wv_ref[...])
    for ci, rows in enumerate(chunks):
        for f in range(ML_VW // L):
            vt_ref[ci, f * L:(f + 1) * L, :] = v[rows, f * L:(f + 1) * L].T.astype(BF16)

    so_ref[...] = jax.nn.sigmoid(_dot(hb, wo_ref[...])).astype(BF16)

    gate_c = _dot(hb, wsc_ref[0])
    val = _dot(hb, wsc_ref[2])
    mixed = _conv3(gate_c * val, csc_ref[...], row_w)
    ysc_ref[...] = (_dot(hb, wsc_ref[1]) * mixed).astype(BF16)

    g = _dot(hb, wg_ref[...]) + bg_ref[...]
    kind = lax.broadcasted_iota(jnp.int32, (L, LANES), 1) % 4
    r_i = lax.broadcasted_iota(jnp.int32, (L, L), 0)
    c_i = lax.broadcasted_iota(jnp.int32, (L, L), 1)
    lower = (c_i <= r_i).astype(BF16)
    upper = (c_i >= r_i).astype(BF16)
    for ci, rows in enumerate(chunks):
        gc = g[rows, :]
        lf = _log_sigmoid(gc)
        out = jnp.where(kind == 1, _tri_sum(lower, lf), jnp.where(kind == 3, _tri_sum(upper, lf), gc))
        for hd in range(ML_HEADS):
            mine = out if hd == 0 else pltpu.roll(out, LANES - 4 * hd, 1)
            g_ref[rows, hd * LANES:(hd + 1) * LANES] = mine
            gt_ref[ci, hd] = mine.T[0:8, :]


def _ev_in(st, x, mod, wts):
    wk, wq, wv, wo, wsc, wg, bg, ck, cq, csc = wts
    tm = st.tm_in
    nck = tm // ML_CHUNK
    n_chunks = st.n_tok // ML_CHUNK
    tok = lambda w: pl.BlockSpec((tm, w), lambda i: (i, 0))
    return pl.pallas_call(
        functools.partial(_ev_in_kernel, row_w=st.row_w),
        grid=(st.n_tok // tm,),
        in_specs=[tok(D_MODEL), st.mod_spec(tm)] + [_resident(w.shape) for w in wts],
        out_specs=[
            tok(ML_KW),
            pl.BlockSpec((nck, ML_KW, ML_CHUNK), lambda i: (i, 0, 0)),
            pl.BlockSpec((nck, ML_VW, ML_CHUNK), lambda i: (i, 0, 0)),
            tok(ML_VW),
            tok(SC_W),
            tok(ML_HEADS * LANES),
            pl.BlockSpec((nck, ML_HEADS, 8, ML_CHUNK), lambda i: (i, 0, 0, 0)),
        ],
        out_shape=[
            jax.ShapeDtypeStruct((st.n_tok, ML_KW), BF16),
            jax.ShapeDtypeStruct((n_chunks, ML_KW, ML_CHUNK), BF16),
            jax.ShapeDtypeStruct((n_chunks, ML_VW, ML_CHUNK), BF16),
            jax.ShapeDtypeStruct((st.n_tok, ML_VW), BF16),
            jax.ShapeDtypeStruct((st.n_tok, SC_W), BF16),
            jax.ShapeDtypeStruct((st.n_tok, ML_HEADS * LANES), F32),
            jax.ShapeDtypeStruct((n_chunks, ML_HEADS, 8, ML_CHUNK), F32),
        ],
        compiler_params=_cparams(("parallel",)),
        name="ev_in",
    )(x, mod, *wts)


def _mlstm_kernel(k_ref, qt_ref, vt_ref, so_ref, g_ref, gt_ref, c0_ref, m0_ref, ng_ref,
                  y_ref, cf_ref, mf_ref, st_scr, nm_scr, *, nch):
    L = ML_CHUNK
    s_i = lax.broadcasted_iota(jnp.int32, (L, L), 0)
    t_i = lax.broadcasted_iota(jnp.int32, (L, L), 1)
    row_8 = lax.broadcasted_iota(jnp.int32, (8, LANES), 0)

    def gate_rows(g8, d):
        ig_row, b_row = g8[2 * d:2 * d + 1, :], g8[2 * d + 1:2 * d + 2, :]
        b_last = b_row[:, L - 1:L] if d == 0 else b_row[:, 0:1]
        return ig_row, b_row, b_last

    def local_pass(c, carry):
        r = pl.multiple_of(c * L, L)
        g8 = gt_ref[c, 0]
        kc = k_ref[pl.ds(r, L), :]
        vtc = vt_ref[c].astype(F32)
        for d in range(2):
            ig_row, b_row, b_last = gate_rows(g8, d)
            a_row = b_last - b_row + ig_row
            m_loc = jnp.max(a_row, axis=-1, keepdims=True)
            w_row = jnp.exp(a_row - m_loc)
            lhs = jnp.concatenate([vtc * w_row, jnp.broadcast_to(w_row, (BF16_ROWS, L))], axis=0)
            st_scr[d, c] = _dot(lhs.astype(BF16), kc)
            nm_scr[d, c] = jnp.where(row_8 == 0, m_loc, b_last)
        return carry

    lax.fori_loop(0, nch, local_pass, 0, unroll=min(ML_UNROLL, nch))

    for d in range(2):
        def stabiliser_scan(it, m_prev, d=d):
            c = it if d == 0 else nch - 1 - it
            blk = nm_scr[d, c]
            m_loc, b_last = blk[0:1, :], blk[1:2, :]
            m_new = jnp.maximum(b_last + m_prev, m_loc)
            e_prev = jnp.exp(b_last + m_prev - m_new)
            e_loc = jnp.exp(m_loc - m_new)
            nm_scr[d, c] = jnp.where(row_8 == 0, m_prev, jnp.where(row_8 == 1, e_prev, e_loc))
            return m_new

        def state_scan(it, cx, d=d):
            c = it if d == 0 else nch - 1 - it
            loc = st_scr[d, c]
            blk = nm_scr[d, c]
            st_scr[d, c] = cx
            return blk[1:2, :] * cx + blk[2:3, :] * loc

        mf_ref[0, 0, d] = lax.fori_loop(0, nch, stabiliser_scan, m0_ref[0, 0, d])
        cf_ref[0, 0, d] = lax.fori_loop(0, nch, state_scan, c0_ref[0, 0, d])

    def output_pass(c, carry):
        r = pl.multiple_of(c * L, L)
        g8 = gt_ref[c, 0]
        gch = g_ref[pl.ds(r, L), :]
        qtc = qt_ref[c]
        vtc = vt_ref[c]
        sraw = _dot(k_ref[pl.ds(r, L), :], qtc)
        h = None
        for d in range(2):
            ig_row, b_row, _ = gate_rows(g8, d)
            src_col = gch[:, 2 * d:2 * d + 1] - gch[:, 2 * d + 1:2 * d + 2]
            visible = (s_i <= t_i) if d == 0 else (s_i >= t_i)
            log_intra = jnp.where(visible, src_col + b_row, -jnp.inf)
            m_prev = nm_scr[d, c][0:1, 0:1]
            log_inter = b_row + m_prev
            m_t = jnp.maximum(log_inter, jnp.max(log_intra, axis=0, keepdims=True))
            s = sraw * jnp.exp(log_intra - m_t)
            w_inter = jnp.exp(log_inter - m_t)
            inter = _dot(st_scr[d, c].astype(BF16), qtc)
            num = _dot(vtc, s.astype(BF16)) + w_inter * inter[0:ML_DV, :]
            den = jnp.sum(s, axis=0, keepdims=True) + w_inter * inter[ML_DV:ML_DV + 1, :]
            hd = num * (1.0 / jnp.maximum(jnp.abs(den), jnp.exp(-m_t)))
            h = hd if h is None else h + hd
        mu = jnp.mean(h, axis=0, keepdims=True)
        hc = h - mu
        var = jnp.mean(hc * hc, axis=0, keepdims=True)
        hn = (hc * lax.rsqrt(var + LN_EPS)).T
        y_ref[pl.ds(r, L), :] = (hn * ng_ref[...] * so_ref[pl.ds(r, L), :].astype(F32)).astype(BF16)
        return carry

    lax.fori_loop(0, nch, output_pass, 0, unroll=min(ML_UNROLL, nch))


def _mlstm(st, k, qt, vt, so, g, gt, state0, norm_g):
    tb = st.t_batch
    nch = tb // ML_CHUNK
    nb, nh = st.n_batch, ML_HEADS
    c0, m0 = state0
    state_specs = [
        pl.BlockSpec((1, 1, 2, ML_DVX, ML_DK), lambda b, h: (b, h, 0, 0, 0)),
        pl.BlockSpec((1, 1, 2, 1, LANES), lambda b, h: (b, h, 0, 0, 0)),
    ]
    y, cf, mf = pl.pallas_call(
        functools.partial(_mlstm_kernel, nch=nch),
        grid=(nb, nh),
        in_specs=[
            pl.BlockSpec((tb, ML_DK), lambda b, h: (b, h)),
            pl.BlockSpec((nch, ML_DK, ML_CHUNK), lambda b, h: (b, h, 0)),
            pl.BlockSpec((nch, ML_DV, ML_CHUNK), lambda b, h: (b, h, 0)),
            pl.BlockSpec((tb, ML_DV), lambda b, h: (b, h)),
            pl.BlockSpec((tb, LANES), lambda b, h: (b, h)),
            pl.BlockSpec((nch, 1, 8, ML_CHUNK), lambda b, h: (b, h, 0, 0)),
        ] + state_specs + [pl.BlockSpec((1, ML_DV), lambda b, h: (0, h))],
        out_specs=[pl.BlockSpec((tb, ML_DV), lambda b, h: (b, h))] + state_specs,
        out_shape=[
            jax.ShapeDtypeStruct((st.n_tok, ML_VW), BF16),
            jax.ShapeDtypeStruct((nb, nh, 2, ML_DVX, ML_DK), F32),
            jax.ShapeDtypeStruct((nb, nh, 2, 1, LANES), F32),
        ],
        scratch_shapes=[
            pltpu.VMEM((2, nch, ML_DVX, ML_DK), F32),
            pltpu.VMEM((2, nch, 8, LANES), F32),
        ],
        compiler_params=_cparams(("parallel", "parallel")),
        name="mlstm",
    )(k, qt, vt, so, g, gt, c0, m0, norm_g)
    return y, (cf, mf)


def _residual_ln(x, gate, y, g_ref, b_ref, alpha):
    return _ln(alpha * x + gate * y, g_ref[...], b_ref[...])


def _ev_out_kernel(yml_ref, ysc_ref, x_ref, mod_ref, w_ref, g_ref, b_ref, o_ref, *, alpha):
    for r in range(0, x_ref.shape[0], OUT_ROWS):
        rows = slice(r, r + OUT_ROWS)
        y = _dot(yml_ref[rows, :], w_ref[0, 0:ML_VW, :]) + _dot(ysc_ref[rows, :], w_ref[0, ML_VW:, :])
        o_ref[rows, :] = _residual_ln(x_ref[rows, :], mod_ref[0, 2:3, :], y, g_ref, b_ref, alpha)


def _ev_out(st, yml, ysc, x, mod, layer, w_out, ln_g, ln_b, alpha):
    tm = st.tm
    tok = lambda w: pl.BlockSpec((tm, w), lambda i: (i, 0))
    return pl.pallas_call(
        functools.partial(_ev_out_kernel, alpha=alpha),
        grid=(st.n_tok // tm,),
        in_specs=[tok(ML_VW), tok(SC_W), tok(D_MODEL), st.mod_spec(tm), _resident_layer(w_out, layer),
                  _resident(ln_g.shape), _resident(ln_b.shape)],
        out_specs=tok(D_MODEL),
        out_shape=jax.ShapeDtypeStruct((st.n_tok, D_MODEL), F32),
        compiler_params=_cparams(("parallel",)),
        name="ev_out",
    )(yml, ysc, x, mod, w_out, ln_g, ln_b)


def _ffn_kernel(x_ref, mod_ref, wa_ref, wg_ref, ca_ref, cg_ref, wdn_ref, g_ref, b_ref, o_ref,
                h_scr, acc, *, row_w, alpha):
    j = pl.program_id(1)
    last = pl.num_programs(1) - 1
    tm = h_scr.shape[0]
    rb = max(row_w, FFN_ROWS)
    blocks = [slice(r, r + rb) for r in range(0, tm, rb)]

    def up_gate(first):
        acts = []
        for rows in blocks:
            if first:
                h = _ada(x_ref[rows, :], mod_ref, 3).astype(BF16)
                h_scr[rows, :] = h
            else:
                h = h_scr[rows, :]
            sg = _silu(_conv3(_dot(h, wg_ref[0]), cg_ref[0], row_w))
            a = _conv3(_dot(h, wa_ref[0]), ca_ref[0], row_w)
            acts.append((sg * a).astype(BF16))
        return acts

    @pl.when(j == 0)
    def _():
        acc[...] = _dot(jnp.concatenate(up_gate(True), axis=0), wdn_ref[0])

    @pl.when((j > 0) & (j < last))
    def _():
        acc[...] += _dot(jnp.concatenate(up_gate(False), axis=0), wdn_ref[0])

    @pl.when(j == last)
    def _():
        for rows, act in zip(blocks, up_gate(False)):
            y = acc[rows, :] + _dot(act, wdn_ref[0])
            o_ref[rows, :] = _residual_ln(x_ref[rows, :], mod_ref[0, 5:6, :], y, g_ref, b_ref, alpha)


def _ffn(st, x, mod, layer, w_up, c_up, w_down, ln_g, ln_b, alpha):
    tm, fc = st.tm, FFN_CHUNK
    nf = D_FF // fc
    return pl.pallas_call(
        functools.partial(_ffn_kernel, row_w=st.row_w, alpha=alpha),
        grid=(st.n_tok // tm, nf),
        in_specs=[
            pl.BlockSpec((tm, D_MODEL), lambda i, j: (i, 0)),
            st.mod_spec(tm),
            pl.BlockSpec((1, D_MODEL, fc), lambda i, j: (layer, 0, j)),
            pl.BlockSpec((1, D_MODEL, fc), lambda i, j: (layer, 0, nf + j)),
            pl.BlockSpec((1, 3, fc), lambda i, j: (layer, 0, j)),
            pl.BlockSpec((1, 3, fc), lambda i, j: (layer, 0, nf + j)),
            pl.BlockSpec((1, fc, D_MODEL), lambda i, j: (layer, j, 0)),
            pl.BlockSpec((1, D_MODEL), lambda i, j: (0, 0)),
            pl.BlockSpec((1, D_MODEL), lambda i, j: (0, 0)),
        ],
        out_specs=pl.BlockSpec((tm, D_MODEL), lambda i, j: (i, 0)),
        out_shape=jax.ShapeDtypeStruct((st.n_tok, D_MODEL), F32),
        scratch_shapes=[pltpu.VMEM((tm, D_MODEL), BF16), pltpu.VMEM((tm, D_MODEL), F32)],
        compiler_params=_cparams(("parallel", "arbitrary")),
        name="ffn",
    )(x, mod, w_up, w_up, c_up, c_up, w_down, ln_g, ln_b)


def _od_in_kernel(x_ref, mod_ref, w_ref, sg_ref, sb_ref, ws_ref, bs_ref, ygm_ref, glu_ref):
    gw = GM_W // GM_GROUPS
    hb = _ada(x_ref[...], mod_ref, 0).astype(BF16)
    part = lambda p: w_ref[0, :, p * GM_W:(p + 1) * GM_W]
    glu_ref[...] = (_dot(hb, part(2)) * jax.nn.sigmoid(_dot(hb, part(3)))).astype(BF16)
    vn = _ln(_gelu(_dot(hb, part(1))), sg_ref[...], sb_ref[...]).astype(BF16)
    u = _gelu(_dot(hb, part(0)))
    for ci in range(x_ref.shape[0] // GM_CHUNK):
        rows = slice(ci * GM_CHUNK, (ci + 1) * GM_CHUNK)
        for g in range(GM_GROUPS):
            cols = slice(g * gw, (g + 1) * gw)
            mixed = _dot(ws_ref[g], vn[rows, cols]) + bs_ref[:, g:g + 1]
            ygm_ref[rows, cols] = (u[rows, cols] * mixed).astype(BF16)


def _od_in(st, x, mod, layer, w_in, sgu_g, sgu_b, w_s, b_s_t):
    tm = st.tm
    tok = lambda w: pl.BlockSpec((tm, w), lambda i: (i, 0))
    out = jax.ShapeDtypeStruct((st.n_tok, GM_W), BF16)
    consts = (sgu_g, sgu_b, w_s, b_s_t)
    return pl.pallas_call(
        _od_in_kernel,
        grid=(st.n_tok // tm,),
        in_specs=[tok(D_MODEL), st.mod_spec(tm), _resident_layer(w_in, layer)] + [_resident(a.shape) for a in consts],
        out_specs=[tok(GM_W), tok(CF_W)],
        out_shape=[out, out],
        compiler_params=_cparams(("parallel",)),
        name="od_in",
    )(x, mod, w_in, *consts)


def _conv_long_kernel(x_ref, w_ref, o_ref, pad, *, stride, tb):
    half = (CF_CONV // 2) * stride
    cb = x_ref.shape[1]
    pad[0:half, :] = jnp.zeros((half, cb), F32)
    pad[half + tb:, :] = jnp.zeros((half, cb), F32)
    pad[half:half + tb, :] = x_ref[...].astype(F32)
    w = w_ref[...]

    def block(r0, aligned):
        acc = jnp.zeros((CONV_ROWS, cb), F32)
        for j in range(CF_CONV):
            start = r0 + j * stride
            acc = acc + w[j:j + 1, :] * pad[pl.ds(pl.multiple_of(start, 8) if aligned else start, CONV_ROWS), :]
        o_ref[pl.ds(r0, CONV_ROWS), :] = acc.astype(o_ref.dtype)

    n_blocks = tb // CONV_ROWS
    if stride % 8 == 0:
        def body(i, carry):
            block(pl.multiple_of(i * CONV_ROWS, CONV_ROWS), True)
            return carry
        lax.fori_loop(0, n_blocks, body, 0)
    else:
        for i in range(n_blocks):
            block(i * CONV_ROWS, False)


def _conv_long(st, x, w):
    tb, cb = st.t_batch, 256
    stride = st.long_stride
    return pl.pallas_call(
        functools.partial(_conv_long_kernel, stride=stride, tb=tb),
        grid=(st.n_batch, CF_W // cb),
        in_specs=[
            pl.BlockSpec((tb, cb), lambda b, j: (b, j)),
            pl.BlockSpec((CF_CONV, cb), lambda b, j: (0, j)),
        ],
        out_specs=pl.BlockSpec((tb, cb), lambda b, j: (b, j)),
        out_shape=jax.ShapeDtypeStruct((st.n_tok, CF_W), BF16),
        scratch_shapes=[pltpu.VMEM((tb + 2 * (CF_CONV // 2) * stride, cb), F32)],
        compiler_params=_cparams(("parallel", "parallel")),
        name="conv_long",
    )(x, w)


def _od_out_kernel(ygm_ref, cv_ref, x_ref, mod_ref, cg_ref, cb_ref, w_ref, g_ref, b_ref, o_ref, *, alpha):
    for r in range(0, x_ref.shape[0], OUT_ROWS):
        rows = slice(r, r + OUT_ROWS)
        ycf = _silu(_ln(cv_ref[rows, :].astype(F32), cg_ref[...], cb_ref[...])).astype(BF16)
        y = _dot(ygm_ref[rows, :], w_ref[0, 0:GM_W, :]) + _dot(ycf, w_ref[0, GM_W:, :])
        o_ref[rows, :] = _residual_ln(x_ref[rows, :], mod_ref[0, 2:3, :], y, g_ref, b_ref, alpha)


def _od_out(st, ygm, cv, x, mod, layer, cf_g, cf_b, w_out, ln_g, ln_b, alpha):
    tm = st.tm
    tok = lambda w: pl.BlockSpec((tm, w), lambda i: (i, 0))
    return pl.pallas_call(
        functools.partial(_od_out_kernel, alpha=alpha),
        grid=(st.n_tok // tm,),
        in_specs=[tok(GM_W), tok(CF_W), tok(D_MODEL), st.mod_spec(tm), _resident(cf_g.shape), _resident(cf_b.shape),
                  _resident_layer(w_out, layer), _resident(ln_g.shape), _resident(ln_b.shape)],
        out_specs=tok(D_MODEL),
        out_shape=jax.ShapeDtypeStruct((st.n_tok, D_MODEL), F32),
        compiler_params=_cparams(("parallel",)),
        name="od_out",
    )(ygm, cv, x, mod, cf_g, cf_b, w_out, ln_g, ln_b)


def _prep_even(w_in, b_gates, conv_k, conv_q, conv_sc):
    wk = w_in[:, 0:ML_KW]
    wq = w_in[:, EV_Q0:EV_O0]
    wv = w_in[:, EV_V0:EV_G0]
    wo = w_in[:, EV_O0:EV_SC0]
    wsc = jnp.stack([w_in[:, EV_SC0 + i * SC_W:EV_SC0 + (i + 1) * SC_W] for i in range(3)])

    def per_head(g):
        lead = g.shape[:-1]
        g = g.reshape(lead + (2, 2, ML_HEADS))
        g = jnp.moveaxis(g, -1, -3).reshape(lead + (ML_GATES,))
        return jnp.pad(g, [(0, 0)] * len(lead) + [(0, LANES - ML_GATES)])

    wg = per_head(w_in[:, EV_G0:EV_G0 + ML_GATES]).astype(BF16)
    bg = per_head(b_gates.reshape(1, ML_GATES))
    return wk, wq, wv, wo, wsc, wg, bg, conv_k, conv_q, conv_sc


def _zero_state(nb):
    return (jnp.zeros((nb, ML_HEADS, 2, ML_DVX, ML_DK), F32),
            jnp.full((nb, ML_HEADS, 2, 1, LANES), M_INIT, F32))


def kernel(x, c, ctx, c_ctx, w_mod, b_mod, ln1_g, ln1_b, ln2_g, ln2_b, ffn_w_up, ffn_conv, ffn_w_down,
           ev_w_in, ev_b_gates, ev_conv_k, ev_conv_q, ev_norm_g, ev_conv_sc, ev_w_out,
           od_w_in, od_sgu_g, od_sgu_b, od_w_s, od_b_s, od_conv_dw, od_cf_g, od_cf_b, od_w_out):
    nb, seq, d = x.shape
    ctx_len = ctx.shape[1]
    depth = w_mod.shape[0]
    assert d == D_MODEL and seq % TM_LATENT == 0 and ctx_len % ML_CHUNK == 0 and nb + 1 <= 8
    alpha = (2.0 * depth) ** 0.25
    last_even = ((depth - 1) // 2) * 2

    lat = _Stream(nb, seq, TM_LATENT, TM_IN, GRID_W, GRID_W, None)
    ctx_tm = ctx_len * max(k for k in range(1, nb + 1) if nb % k == 0 and k * ctx_len <= TM_LATENT)
    cst = _Stream(nb, ctx_len, ctx_tm, ctx_len, ctx_len, 1, nb)

    c8 = jnp.zeros((8, d), F32).at[:nb].set(c).at[nb].set(c_ctx)
    mod_all = _modulation(c8, w_mod, b_mod).reshape(depth, 8, 6, d)

    xs = x.reshape(nb * seq, d)
    cs = ctx.reshape(nb * ctx_len, d)
    row = lambda v: v.reshape(1, -1)

    ffn_up_b, ffn_down_b = ffn_w_up.astype(BF16), ffn_w_down.astype(BF16)
    ev_in_b, ev_out_b = ev_w_in.astype(BF16), ev_w_out.astype(BF16)
    od_in_b, od_out_b, od_ws_b = od_w_in.astype(BF16), od_w_out.astype(BF16), od_w_s.astype(BF16)

    for l in range(depth):
        ctx_out = l < last_even
        j = l // 2
        mod = mod_all[l]
        g1, b1, g2, b2 = row(ln1_g[l]), row(ln1_b[l]), row(ln2_g[l]), row(ln2_b[l])

        if l % 2 == 0:
            wts = _prep_even(ev_in_b[j], ev_b_gates[j], ev_conv_k[j], ev_conv_q[j], ev_conv_sc[j])
            norm_g = row(ev_norm_g[j])

            def even_mixer(st, z, state0, want_y):
                k, qt, vt, so, ysc, g, gt = _ev_in(st, z, mod, wts)
                yml, fin = _mlstm(st, k, qt, vt, so, g, gt, state0, norm_g)
                if not want_y:
                    return None, fin
                return _ev_out(st, yml, ysc, z, mod, j, ev_out_b, g1, b1, alpha), fin

            c1, fin = even_mixer(cst, cs, _zero_state(nb), ctx_out)
            x1, _ = even_mixer(lat, xs, fin, True)
        else:
            b_s_t = od_b_s[j].T

            def odd_mixer(st, z):
                ygm, glu = _od_in(st, z, mod, j, od_in_b, row(od_sgu_g[j]), row(od_sgu_b[j]), od_ws_b[j], b_s_t)
                cv = _conv_long(st, glu, od_conv_dw[j])
                return _od_out(st, ygm, cv, z, mod, j, row(od_cf_g[j]), row(od_cf_b[j]), od_out_b, g1, b1, alpha)

            x1 = odd_mixer(lat, xs)
            c1 = odd_mixer(cst, cs) if ctx_out else None

        xs = _ffn(lat, x1, mod, l, ffn_up_b, ffn_conv, ffn_down_b, g2, b2, alpha)
        if ctx_out:
            cs = _ffn(cst, c1, mod, l, ffn_up_b, ffn_conv, ffn_down_b, g2, b2, alpha)

    return xs.reshape(nb, seq, d)
```
